```python
import math
import jax
import jax.numpy as jnp
from jax import lax
import numpy as np


D_MODEL = 1024
BATCH = 4
SEQ = 4096
DEPTH = 2
DEC_BATCH = 8
DEC_SEQ = 16
PAST_LEN = 4096

CHUNK = 64
EPS = 1e-6
CONV_W = 4
NH_A = 4
DK_A = 128
DV_A = 256
D_A = NH_A * DV_A
NH_B = 16
HEADDIM_B = 64
D_B = NH_B * HEADDIM_B
D_STATE = 128
NG_B = 2
HPG_B = NH_B // NG_B
CONV_DIM_B = D_B + 2 * NG_B * D_STATE
NH_C = 8
DK_C = 128
DV_C = 128
D_CK = NH_C * DK_C
D_CV = NH_C * DV_C
CONV_DIM_C = 2 * D_CK + D_CV
D_FF = 2816

SIZES_AB = (NH_A * DK_A, NH_A * DK_A, D_A, D_A, NH_A, NH_A, D_B, CONV_DIM_B, NH_B)
D_IN_AB = 2 * NH_A * DK_A + 2 * D_A + 2 * NH_A + D_B + CONV_DIM_B + NH_B
SIZES_C = (CONV_DIM_C, D_CV, NH_C, NH_C)
D_IN_C = CONV_DIM_C + D_CV + 2 * NH_C

kernel_name = 'hybrid_mlstm_ssd_gdn_macaron_stream_step'


def f32(t):
    return t.astype(jnp.float32)


def rmsnorm(x, g):
    xf = f32(x)
    y = xf * lax.rsqrt(jnp.mean(xf * xf, axis=-1, keepdims=True) + EPS)
    return (y * f32(g)).astype(x.dtype)


def l2norm(x):
    return x * lax.rsqrt(jnp.sum(x * x, axis=-1, keepdims=True) + EPS)


def swiglu(x, w_gate, w_up, w_down):
    return (jax.nn.silu(x @ w_gate) * (x @ w_up)) @ w_down


def split_cols(a, sizes):
    out, off = [], 0
    for s in sizes:
        out.append(a[..., off:off + s])
        off += s
    return out


def tril(L, k=0):
    return jnp.tril(jnp.ones((L, L), dtype=bool), k)


def causal_conv(x, buf, w, b=None):
    xp = jnp.concatenate([buf.astype(x.dtype), x], axis=1)
    y = lax.conv_general_dilated(f32(xp), f32(w)[:, None, :], (1,), 'VALID',
                                 dimension_numbers=('NWC', 'WIO', 'NWC'),
                                 feature_group_count=x.shape[-1])
    if b is not None:
        y = y + f32(b)
    return y, xp[:, xp.shape[1] - (CONV_W - 1):]


def to_chunks(a):
    return jnp.moveaxis(a.reshape(a.shape[0], a.shape[1] // CHUNK, CHUNK, *a.shape[2:]), 1, 0)


def from_chunks(a):
    a = jnp.moveaxis(a, 0, 1)
    return a.reshape(a.shape[0], a.shape[1] * a.shape[2], *a.shape[3:])


def chunk_scan(step, state, xs):
    if xs[0].shape[1] <= CHUNK:
        return step(state, xs)
    state, ys = lax.scan(step, state, tuple(to_chunks(t) for t in xs))
    return state, from_chunks(ys)


def mlstm_step(state, xs):
    C, n, m = state
    q, k, v, ig, lf = xs
    L = q.shape[1]
    causal = tril(L)[None, :, :, None]
    b = jnp.cumsum(lf, axis=1)
    dmat = jnp.where(causal, b[:, :, None, :] - b[:, None, :, :] + ig[:, None, :, :], -jnp.inf)
    inter = b + m[:, None, :]
    m_s = jnp.maximum(inter, dmat.max(axis=2))
    scores = jnp.einsum('bshd,brhd->bsrh', q, k) * jnp.exp(dmat - m_s[:, :, None, :])
    w_inter = jnp.exp(inter - m_s)
    num = jnp.einsum('bsrh,brhe->bshe', scores, v) + w_inter[..., None] * jnp.einsum('bshd,bhde->bshe', q, C)
    den = scores.sum(axis=2) + w_inter * jnp.einsum('bshd,bhd->bsh', q, n)
    h = num / jnp.maximum(jnp.abs(den), jnp.exp(-m_s))[..., None]
    bL = b[:, -1]
    dl = bL[:, None, :] - b + ig
    m_new = jnp.maximum(bL + m, dl.max(axis=1))
    wr = jnp.exp(dl - m_new[:, None, :])
    decay = jnp.exp(bL + m - m_new)
    C = decay[..., None, None] * C + jnp.einsum('brh,brhd,brhe->bhde', wr, k, v)
    n = decay[..., None] * n + jnp.einsum('brh,brhd->bhd', wr, k)
    return (C, n, m_new), h


def ssd_step(S, xs):
    x, bm, cm, dt, a = xs
    L = x.shape[1]
    causal = tril(L)[None, :, :, None, None]
    b = jnp.cumsum(a, axis=1)
    seg = jnp.exp(jnp.where(causal, b[:, :, None] - b[:, None], -jnp.inf))
    cb = jnp.einsum('bsgn,brgn->bsrg', cm, bm)
    y = (jnp.einsum('bsrgj,brgjp->bsgjp', cb[..., None] * seg, dt[..., None] * x)
         + jnp.exp(b)[..., None] * jnp.einsum('bsgn,bgjpn->bsgjp', cm, S))
    bL = b[:, -1]
    S = (jnp.exp(bL)[..., None, None] * S
         + jnp.einsum('brgj,brgjp,brgn->bgjpn', jnp.exp(bL[:, None] - b) * dt, x, bm))
    return S, y


def gdn_step(S, xs):
    q, k, v, beta, g = xs
    L = q.shape[1]
    gam = jnp.swapaxes(jnp.cumsum(g, axis=1), 1, 2)
    dec = jnp.exp(jnp.where(tril(L), gam[..., :, None] - gam[..., None, :], -jnp.inf))
    qh, kh, vh = jnp.swapaxes(q, 1, 2), jnp.swapaxes(k, 1, 2), jnp.swapaxes(v, 1, 2)
    bh = jnp.swapaxes(beta, 1, 2)[..., None]
    a_mat = jnp.where(tril(L, -1), bh * dec * (kh @ jnp.swapaxes(kh, -1, -2)), 0.0)
    rhs = jnp.concatenate([bh * vh, bh * jnp.exp(gam)[..., None] * kh], axis=-1)
    sol = lax.linalg.triangular_solve(a_mat, rhs, left_side=True, lower=True, unit_diagonal=True)
    u = sol[..., :DV_C] - sol[..., DV_C:] @ S
    o = jnp.exp(gam)[..., None] * (qh @ S) + ((qh @ jnp.swapaxes(kh, -1, -2)) * dec) @ u
    gL = gam[..., -1]
    S = (jnp.exp(gL)[..., None, None] * S
         + jnp.einsum('bhr,bhrd,bhre->bhde', jnp.exp(gL[..., None] - gam), kh, u))
    return S, jnp.swapaxes(o, 1, 2)


def mixer_ab(h, state, w_in, w_out, b_i, b_f, mlstm_norm, conv_w, conv_b, dt_bias, a_log, d_skip, ssd_norm):
    bsz, T, _ = h.shape
    C0, n0, m0, S0, buf0 = state
    q, k, v, o, ig, fg, z, xbc, dt = split_cols(h @ w_in, SIZES_AB)
    q = f32(q).reshape(bsz, T, NH_A, DK_A)
    k = f32(k).reshape(bsz, T, NH_A, DK_A) * (DK_A ** -0.5)
    v = f32(v).reshape(bsz, T, NH_A, DV_A)
    ig = f32(ig) + f32(b_i)
    lf = jax.nn.log_sigmoid(f32(fg) + f32(b_f))
    (C1, n1, m1), h_a = chunk_scan(mlstm_step, (f32(C0), f32(n0), f32(m0)), (q, k, v, ig, lf))
    h_a = jax.nn.sigmoid(f32(o)) * rmsnorm(h_a, mlstm_norm).reshape(bsz, T, D_A)
    xbc, buf1 = causal_conv(xbc, buf0, conv_w, conv_b)
    xs, bm, cm = split_cols(jax.nn.silu(xbc), (D_B, NG_B * D_STATE, NG_B * D_STATE))
    dt = jax.nn.softplus(f32(dt) + f32(dt_bias))
    a = -jnp.exp(f32(a_log)) * dt
    x5 = xs.reshape(bsz, T, NG_B, HPG_B, HEADDIM_B)
    S1, y = chunk_scan(ssd_step, f32(S0).reshape(bsz, NG_B, HPG_B, HEADDIM_B, D_STATE),
                       (x5, bm.reshape(bsz, T, NG_B, D_STATE), cm.reshape(bsz, T, NG_B, D_STATE),
                        dt.reshape(bsz, T, NG_B, HPG_B), a.reshape(bsz, T, NG_B, HPG_B)))
    y = (y + f32(d_skip).reshape(NG_B, HPG_B, 1) * x5).reshape(bsz, T, D_B) * jax.nn.silu(f32(z))
    y = rmsnorm(y.reshape(bsz, T, NG_B, D_B // NG_B), ssd_norm.reshape(NG_B, D_B // NG_B)).reshape(bsz, T, D_B)
    out = jnp.concatenate([h_a, y], axis=-1).astype(h.dtype) @ w_out
    return out, (C1, n1, m1, S1.reshape(bsz, NH_B, HEADDIM_B, D_STATE), buf1)


def mixer_c(h, state, w_in, w_out, conv_w, dt_bias, a_log, gdn_norm):
    bsz, T, _ = h.shape
    S0, buf0 = state
    qkv, z, b, a = split_cols(h @ w_in, SIZES_C)
    qkv, buf1 = causal_conv(qkv, buf0, conv_w)
    q, k, v = split_cols(jax.nn.silu(qkv), (D_CK, D_CK, D_CV))
    q = l2norm(q.reshape(bsz, T, NH_C, DK_C)) * (DK_C ** -0.5)
    k = l2norm(k.reshape(bsz, T, NH_C, DK_C))
    v = v.reshape(bsz, T, NH_C, DV_C)
    beta = jax.nn.sigmoid(f32(b))
    g = -jnp.exp(f32(a_log)) * jax.nn.softplus(f32(a) + f32(dt_bias))
    S1, o = chunk_scan(gdn_step, f32(S0), (q, k, v, beta, g))
    o = rmsnorm(o, gdn_norm) * jax.nn.silu(f32(z).reshape(bsz, T, NH_C, DV_C))
    return o.reshape(bsz, T, D_CV).astype(h.dtype) @ w_out, (S1, buf1)


def trunk(x, ab_state, c_state, norm_g, norm_f, ffn_w_gate, ffn_w_up, ffn_w_down,
          ab_w_in, ab_w_out, mlstm_b_i, mlstm_b_f, mlstm_norm, ssd_conv_w, ssd_conv_b,
          ssd_dt_bias, ssd_a_log, ssd_d, ssd_norm, gdn_w_in, gdn_w_out, gdn_conv_w,
          gdn_dt_bias, gdn_a_log, gdn_norm):
    for l in range(DEPTH):
        x = x + 0.5 * swiglu(rmsnorm(x, norm_g[l, 0]), ffn_w_gate[l, 0], ffn_w_up[l, 0], ffn_w_down[l, 0])
        hn = rmsnorm(x, norm_g[l, 1])
        if l % 2 == 0:
            out, ab_state = mixer_ab(hn, ab_state, ab_w_in, ab_w_out, mlstm_b_i, mlstm_b_f, mlstm_norm,
                                     ssd_conv_w, ssd_conv_b, ssd_dt_bias, ssd_a_log, ssd_d, ssd_norm)
        else:
            out, c_state = mixer_c(hn, c_state, gdn_w_in, gdn_w_out, gdn_conv_w, gdn_dt_bias, gdn_a_log, gdn_norm)
        x = x + out
        x = x + 0.5 * swiglu(rmsnorm(x, norm_g[l, 2]), ffn_w_gate[l, 1], ffn_w_up[l, 1], ffn_w_down[l, 1])
    ab_state = tuple(s.astype(x.dtype) for s in ab_state)
    c_state = tuple(s.astype(x.dtype) for s in c_state)
    return rmsnorm(x, norm_f), ab_state, c_state


def _dt_bias(key, n):
    dt = jnp.exp(jax.random.uniform(key, (n,), jnp.float32, math.log(1e-3), math.log(1e-1)))
    return dt + jnp.log(-jnp.expm1(-dt))


def setup_inputs(seed: int = 0) -> dict:
    key = jax.random.key(seed)
    ks = jax.random.split(key, 32)

    def nrm(k, shape, scale):
        return scale * jax.random.normal(k, shape, jnp.float32)

    return {
        'x_prompt': nrm(ks[0], (BATCH, SEQ, D_MODEL), 1.0),
        'x_sample': nrm(ks[1], (DEC_BATCH, DEC_SEQ, D_MODEL), 1.0),
        'state_mlstm_C': nrm(ks[2], (DEC_BATCH, NH_A, DK_A, DV_A), 0.1),
        'state_mlstm_n': nrm(ks[3], (DEC_BATCH, NH_A, DK_A), 0.1),
        'state_mlstm_m': nrm(ks[4], (DEC_BATCH, NH_A), 0.5),
        'state_ssd': nrm(ks[5], (DEC_BATCH, NH_B, HEADDIM_B, D_STATE), 0.1),
        'cache_ssd_conv': nrm(ks[6], (DEC_BATCH, CONV_W - 1, CONV_DIM_B), 1.0),
        'state_gdn': nrm(ks[7], (DEC_BATCH, NH_C, DK_C, DV_C), 0.1),
        'cache_gdn_conv': nrm(ks[8], (DEC_BATCH, CONV_W - 1, CONV_DIM_C), 1.0),
        'norm_g': 1.0 + nrm(ks[9], (DEPTH, 3, D_MODEL), 0.02),
        'norm_f': 1.0 + nrm(ks[10], (D_MODEL,), 0.02),
        'ffn_w_gate': nrm(ks[11], (DEPTH, 2, D_MODEL, D_FF), D_MODEL ** -0.5),
        'ffn_w_up': nrm(ks[12], (DEPTH, 2, D_MODEL, D_FF), D_MODEL ** -0.5),
        'ffn_w_down': nrm(ks[13], (DEPTH, 2, D_FF, D_MODEL), D_FF ** -0.5),
        'ab_w_in': nrm(ks[14], (D_MODEL, D_IN_AB), D_MODEL ** -0.5),
        'ab_w_out': nrm(ks[15], (D_A + D_B, D_MODEL), (D_A + D_B) ** -0.5),
        'mlstm_b_i': nrm(ks[16], (NH_A,), 0.1),
        'mlstm_b_f': jnp.linspace(3.0, 6.0, NH_A, dtype=jnp.float32) + nrm(ks[17], (NH_A,), 0.1),
        'mlstm_norm': 1.0 + nrm(ks[18], (NH_A, DV_A), 0.02),
        'ssd_conv_w': nrm(ks[19], (CONV_W, CONV_DIM_B), CONV_W ** -0.5),
        'ssd_conv_b': nrm(ks[20], (CONV_DIM_B,), 0.02),
        'ssd_dt_bias': _dt_bias(ks[21], NH_B),
        'ssd_a_log': jnp.log(jax.random.uniform(ks[22], (NH_B,), jnp.float32, 1.0, 16.0)),
        'ssd_d': 1.0 + nrm(ks[23], (NH_B,), 0.1),
        'ssd_norm': 1.0 + nrm(ks[24], (D_B,), 0.02),
        'gdn_w_in': nrm(ks[25], (D_MODEL, D_IN_C), D_MODEL ** -0.5),
        'gdn_w_out': nrm(ks[26], (D_CV, D_MODEL), D_CV ** -0.5),
        'gdn_conv_w': nrm(ks[27], (CONV_W, CONV_DIM_C), CONV_W ** -0.5),
        'gdn_dt_bias': _dt_bias(ks[28], NH_C),
        'gdn_a_log': jnp.log(jax.random.uniform(ks[29], (NH_C,), jnp.float32, 1.0, 16.0)),
        'gdn_norm': 1.0 + nrm(ks[30], (DV_C,), 0.02),
    }


def reference(x_prompt, x_sample, state_mlstm_C, state_mlstm_n, state_mlstm_m, state_ssd,
              cache_ssd_conv, state_gdn, cache_gdn_conv, norm_g, norm_f, ffn_w_gate, ffn_w_up,
              ffn_w_down, ab_w_in, ab_w_out, mlstm_b_i, mlstm_b_f, mlstm_norm, ssd_conv_w,
              ssd_conv_b, ssd_dt_bias, ssd_a_log, ssd_d, ssd_norm, gdn_w_in, gdn_w_out,
              gdn_conv_w, gdn_dt_bias, gdn_a_log, gdn_norm):
    weights = (norm_g, norm_f, ffn_w_gate, ffn_w_up, ffn_w_down, ab_w_in, ab_w_out, mlstm_b_i,
               mlstm_b_f, mlstm_norm, ssd_conv_w, ssd_conv_b, ssd_dt_bias, ssd_a_log, ssd_d,
               ssd_norm, gdn_w_in, gdn_w_out, gdn_conv_w, gdn_dt_bias, gdn_a_log, gdn_norm)
    bp = x_prompt.shape[0]
    fdt = jnp.float32
    ab0 = (jnp.zeros((bp, NH_A, DK_A, DV_A), fdt), jnp.zeros((bp, NH_A, DK_A), fdt),
           jnp.zeros((bp, NH_A), fdt), jnp.zeros((bp, NH_B, HEADDIM_B, D_STATE), fdt),
           jnp.zeros((bp, CONV_W - 1, CONV_DIM_B), x_prompt.dtype))
    c0 = (jnp.zeros((bp, NH_C, DK_C, DV_C), fdt), jnp.zeros((bp, CONV_W - 1, CONV_DIM_C), x_prompt.dtype))
    y_prompt, (p_C, p_n, p_m, p_ssd, p_ssd_conv), (p_gdn, p_gdn_conv) = trunk(x_prompt, ab0, c0, *weights)
    y_sample, (s_C, s_n, s_m, s_ssd, s_ssd_conv), (s_gdn, s_gdn_conv) = trunk(
        x_sample, (state_mlstm_C, state_mlstm_n, state_mlstm_m, state_ssd, cache_ssd_conv),
        (state_gdn, cache_gdn_conv), *weights)
    return (y_prompt, y_sample, p_C, p_n, p_m, p_ssd, p_ssd_conv, p_gdn, p_gdn_conv,
            s_C, s_n, s_m, s_ssd, s_ssd_conv, s_gdn, s_gdn_conv)
```

```python
import functools
import math

import jax
import jax.numpy as jnp
from jax import lax
from jax.experimental import pallas as pl
from jax.experimental.pallas import tpu as pltpu

D_MODEL = 1024
EPS = 1e-6
CONV_W = 4
NH_A, DK_A, DV_A = 4, 128, 256
D_A = NH_A * DV_A
NH_B, HEADDIM_B, D_STATE, NG_B = 16, 64, 128, 2
D_B = NH_B * HEADDIM_B
HPG_B = NH_B // NG_B
CONV_DIM_B = D_B + 2 * NG_B * D_STATE
NH_C, DK_C, DV_C = 8, 128, 128
D_CK = NH_C * DK_C
D_CV = NH_C * DV_C
CONV_DIM_C = 2 * D_CK + D_CV
D_FF = 2816
SIZES_AB = (NH_A * DK_A, NH_A * DK_A, D_A, D_A, NH_A, NH_A, D_B, CONV_DIM_B, NH_B)
SIZES_C = (CONV_DIM_C, D_CV, NH_C, NH_C)

LANES = 128
CONV_PAD = 8
VMEM_LIMIT = 56 * 1024 * 1024
FFN_COLS = 256
INV_BLOCK = 16

_MXU_DTYPE = jnp.bfloat16
_F32 = jnp.float32
_HI = lax.Precision.HIGHEST


def _mm(a, b):
    return jnp.dot(a.astype(_MXU_DTYPE), b.astype(_MXU_DTYPE), preferred_element_type=_F32)


def _mm_nt(a, b):
    return lax.dot_general(a.astype(_MXU_DTYPE), b.astype(_MXU_DTYPE),
                           (((1,), (1,)), ((), ())), preferred_element_type=_F32)


def _mm_tn(a, b):
    return lax.dot_general(a.astype(_MXU_DTYPE), b.astype(_MXU_DTYPE),
                           (((0,), (0,)), ((), ())), preferred_element_type=_F32)


def _mm_hi(a, b):
    return jnp.dot(a, b, precision=_HI, preferred_element_type=_F32)


def _rms(x, g):
    return x * lax.rsqrt(jnp.mean(x * x, axis=-1, keepdims=True) + EPS) * g


def _sigmoid(x):
    return 1.0 / (1.0 + jnp.exp(-x))


def _silu(x):
    return x * _sigmoid(x)


def _softplus(x):
    return jnp.maximum(x, 0.0) + jnp.log1p(jnp.exp(-jnp.abs(x)))


def _log_sigmoid(x):
    return -_softplus(-x)


def _iota2(shape, dim):
    return lax.broadcasted_iota(jnp.int32, shape, dim)


def _cumsum_rows(x):
    n = x.shape[0]
    tri = (_iota2((n, n), 0) >= _iota2((n, n), 1)).astype(_F32)
    return _mm_hi(tri, x)


def _full_spec(shape):
    nd = len(shape)
    return pl.BlockSpec(shape, lambda *_: (0,) * nd)


def _row_tile(m, cap):
    t = min(m, cap)
    assert m % t == 0
    return t


def _params(sem):
    return pltpu.CompilerParams(dimension_semantics=sem, vmem_limit_bytes=VMEM_LIMIT)


def _ffn_kernel(x_ref, g_ref, wg_ref, wu_ref, wd_ref, *rest, final_norm):
    if final_norm:
        gf_ref, o_ref, h_ref = rest
    else:
        o_ref, h_ref = rest
    x = x_ref[...]
    hn = _rms(x, g_ref[...]).astype(_MXU_DTYPE)
    for c in range(D_FF // FFN_COLS):
        cols = slice(c * FFN_COLS, (c + 1) * FFN_COLS)
        gate = jnp.dot(hn, wg_ref[:, cols], preferred_element_type=_F32)
        up = jnp.dot(hn, wu_ref[:, cols], preferred_element_type=_F32)
        h_ref[:, cols] = (_silu(gate) * up).astype(_MXU_DTYPE)
    y = x + 0.5 * jnp.dot(h_ref[...], wd_ref[...], preferred_element_type=_F32)
    if final_norm:
        y = _rms(y, gf_ref[...])
    o_ref[...] = y


def _ffn(x, g, wg, wu, wd, final_g=None):
    m = x.shape[0]
    tm = _row_tile(m, 512)
    row = pl.BlockSpec((tm, D_MODEL), lambda i: (i, 0))
    ins = [x, g.reshape(1, D_MODEL), wg, wu, wd]
    specs = [row, _full_spec((1, D_MODEL)), _full_spec(wg.shape), _full_spec(wu.shape), _full_spec(wd.shape)]
    if final_g is not None:
        ins.append(final_g.reshape(1, D_MODEL))
        specs.append(_full_spec((1, D_MODEL)))
    return pl.pallas_call(
        functools.partial(_ffn_kernel, final_norm=final_g is not None),
        grid=(m // tm,),
        in_specs=specs,
        out_specs=row,
        out_shape=jax.ShapeDtypeStruct((m, D_MODEL), _F32),
        scratch_shapes=[pltpu.VMEM((tm, D_FF), _MXU_DTYPE)],
        compiler_params=_params(("parallel",)),
    )(*ins)


def _norm_proj_kernel(x_ref, g_ref, *refs, n_out):
    w_refs, o_refs = refs[:n_out], refs[n_out:]
    hn = _rms(x_ref[...], g_ref[...]).astype(_MXU_DTYPE)
    for w_ref, o_ref in zip(w_refs, o_refs):
        o_ref[...] = jnp.dot(hn, w_ref[...], preferred_element_type=_F32)


def _norm_proj(x, g, ws):
    m = x.shape[0]
    tm = _row_tile(m, 256)
    row = lambda n: pl.BlockSpec((tm, n), lambda i: (i, 0))
    return pl.pallas_call(
        functools.partial(_norm_proj_kernel, n_out=len(ws)),
        grid=(m // tm,),
        in_specs=[row(D_MODEL), _full_spec((1, D_MODEL))] + [_full_spec(w.shape) for w in ws],
        out_specs=[row(w.shape[1]) for w in ws],
        out_shape=[jax.ShapeDtypeStruct((m, w.shape[1]), _F32) for w in ws],
        compiler_params=_params(("parallel",)),
    )(x, g.reshape(1, D_MODEL), *ws)


def _out_proj_kernel(x_ref, *refs, n_in):
    a_refs, w_refs, o_ref = refs[:n_in], refs[n_in:2 * n_in], refs[2 * n_in]
    y = x_ref[...]
    for a_ref, w_ref in zip(a_refs, w_refs):
        y = y + jnp.dot(a_ref[...].astype(_MXU_DTYPE), w_ref[...], preferred_element_type=_F32)
    o_ref[...] = y


def _out_proj(x, acts, ws):
    m = x.shape[0]
    tm = _row_tile(m, 512)
    row = lambda n: pl.BlockSpec((tm, n), lambda i: (i, 0))
    return pl.pallas_call(
        functools.partial(_out_proj_kernel, n_in=len(acts)),
        grid=(m // tm,),
        in_specs=[row(D_MODEL)] + [row(a.shape[1]) for a in acts] + [_full_spec(w.shape) for w in ws],
        out_specs=row(D_MODEL),
        out_shape=jax.ShapeDtypeStruct((m, D_MODEL), _F32),
        compiler_params=_params(("parallel",)),
    )(x, *acts, *ws)


def _seq_spec(L, n):
    return pl.BlockSpec((None, L, n), lambda b, c: (b, c, 0))


def _state_spec(shape):
    nd = len(shape)
    return pl.BlockSpec((None,) + tuple(shape[1:]), lambda b, c: (b,) + (0,) * (nd - 1))


def _conv_silu(x_ref, w_ref, bias, buf0_ref, win_ref, tail_ref, L):
    lo = CONV_PAD - (CONV_W - 1)

    @pl.when(pl.program_id(1) == 0)
    def _():
        win_ref[lo:CONV_PAD, :] = buf0_ref[...]

    win_ref[CONV_PAD:CONV_PAD + L, :] = x_ref[...]
    y = w_ref[0:1, :] * win_ref[lo:lo + L, :]
    for i in range(1, CONV_W):
        y = y + w_ref[i:i + 1, :] * win_ref[lo + i:lo + i + L, :]
    if bias is not None:
        y = y + bias
    tail = win_ref[L + lo:L + CONV_PAD, :]
    win_ref[lo:CONV_PAD, :] = tail
    tail_ref[...] = tail
    return _silu(y)


def _mlstm_kernel(q_ref, k_ref, v_ref, o_ref, g_ref, bi_ref, bf_ref, nw_ref, c0_ref, n0_ref, m0_ref,
                  h_ref, c_ref, n_ref, m_ref, *, L):
    @pl.when(pl.program_id(1) == 0)
    def _():
        c_ref[...] = c0_ref[...]
        n_ref[...] = n0_ref[...]
        m_ref[...] = m0_ref[...]

    gates = g_ref[...]
    ig = gates + bi_ref[...]
    b = _cumsum_rows(_log_sigmoid(gates + bf_ref[...]))
    ig_t, b_t = ig.T, b.T
    causal = _iota2((L, L), 0) >= _iota2((L, L), 1)
    lane = _iota2((1, NH_A), 1)
    m_old = m_ref[...]
    m_next = m_old
    for h in range(NH_A):
        q = q_ref[:, h * DK_A:(h + 1) * DK_A]
        k = k_ref[:, h * DK_A:(h + 1) * DK_A] * (DK_A ** -0.5)
        v = v_ref[:, h * DV_A:(h + 1) * DV_A]
        b_c = b[:, NH_A + h:NH_A + h + 1]
        ig_c = ig[:, h:h + 1]
        row = ig_t[h:h + 1, :] - b_t[NH_A + h:NH_A + h + 1, :]
        m_prev = m_old[:, h:h + 1]
        dmat = jnp.where(causal, b_c + row, -jnp.inf)
        inter = b_c + m_prev
        m_s = jnp.maximum(inter, jnp.max(dmat, axis=1, keepdims=True))
        scores = _mm_nt(q, k) * jnp.exp(dmat - m_s)
        w_inter = jnp.exp(inter - m_s)
        num = _mm(scores, v) + w_inter * _mm(q, c_ref[h])
        den = (jnp.sum(scores, axis=1, keepdims=True)
               + w_inter * jnp.sum(q * n_ref[h:h + 1, :], axis=1, keepdims=True))
        hh = num / jnp.maximum(jnp.abs(den), jnp.exp(-m_s))
        hh = _rms(hh, nw_ref[h:h + 1, :])
        h_ref[:, h * DV_A:(h + 1) * DV_A] = _sigmoid(o_ref[:, h * DV_A:(h + 1) * DV_A]) * hh
        b_last = b_c[L - 1:L, :]
        dl = b_last - b_c + ig_c
        m_new = jnp.maximum(b_last + m_prev, jnp.max(dl, axis=0, keepdims=True))
        kw = k * jnp.exp(dl - m_new)
        decay = jnp.exp(b_last + m_prev - m_new)
        c_ref[h] = decay * c_ref[h] + _mm_tn(kw, v)
        n_ref[h:h + 1, :] = decay * n_ref[h:h + 1, :] + jnp.sum(kw, axis=0, keepdims=True)
        m_next = jnp.where(lane == h, m_new, m_next)
    m_ref[...] = m_next


def _mlstm(q, k, v, o, gates, bi_row, bf_row, norm_w, c0, n0, m0, L):
    bsz, T, _ = q.shape
    m0 = m0.reshape(bsz, 1, NH_A)
    row128 = _full_spec((1, LANES))
    h, c1, n1, m1 = pl.pallas_call(
        functools.partial(_mlstm_kernel, L=L),
        grid=(bsz, T // L),
        in_specs=[_seq_spec(L, NH_A * DK_A), _seq_spec(L, NH_A * DK_A), _seq_spec(L, D_A), _seq_spec(L, D_A),
                  _seq_spec(L, LANES), row128, row128, _full_spec((NH_A, DV_A)),
                  _state_spec(c0.shape), _state_spec(n0.shape), _state_spec(m0.shape)],
        out_specs=[_seq_spec(L, D_A), _state_spec(c0.shape), _state_spec(n0.shape), _state_spec(m0.shape)],
        out_shape=[jax.ShapeDtypeStruct((bsz, T, D_A), _F32),
                   jax.ShapeDtypeStruct(c0.shape, _F32), jax.ShapeDtypeStruct(n0.shape, _F32),
                   jax.ShapeDtypeStruct(m0.shape, _F32)],
        compiler_params=_params(("parallel", "arbitrary")),
    )(q, k, v, o, gates, bi_row, bf_row, norm_w, c0, n0, m0)
    return h, c1, n1, m1.reshape(bsz, NH_A)


def _ssd_kernel(x_ref, z_ref, g_ref, cw_ref, cb_ref, dtb_ref, alog_ref, dsk_ref, nw_ref, s0_ref, buf0_ref,
                y_ref, s_ref, tail_ref, win_ref, *, L):
    @pl.when(pl.program_id(1) == 0)
    def _():
        s_ref[...] = s0_ref[...]

    act = _conv_silu(x_ref, cw_ref, cb_ref[...], buf0_ref, win_ref, tail_ref, L)
    dt = _softplus(g_ref[...] + dtb_ref[...])
    b = _cumsum_rows(-jnp.exp(alog_ref[...]) * dt)
    eb = jnp.exp(b)
    b_t = b.T
    causal = _iota2((L, L), 0) >= _iota2((L, L), 1)
    lane0 = 2 * NH_A
    for g in range(NG_B):
        bm = act[:, D_B + g * D_STATE:D_B + (g + 1) * D_STATE]
        cm = act[:, D_B + (NG_B + g) * D_STATE:D_B + (NG_B + g + 1) * D_STATE]
        cb = _mm_nt(cm, bm)
        s_g = s_ref[g * HPG_B:(g + 1) * HPG_B].reshape(HPG_B * HEADDIM_B, D_STATE)
        y_inter = _mm_nt(cm, s_g)
        ys, xws = [], []
        for j in range(HPG_B):
            hd = g * HPG_B + j
            ln = lane0 + hd
            x = act[:, hd * HEADDIM_B:(hd + 1) * HEADDIM_B]
            b_c = b[:, ln:ln + 1]
            dt_c = dt[:, ln:ln + 1]
            seg = jnp.exp(jnp.where(causal, b_c - b_t[ln:ln + 1, :], -jnp.inf))
            y = _mm(cb * seg, dt_c * x) + eb[:, ln:ln + 1] * y_inter[:, j * HEADDIM_B:(j + 1) * HEADDIM_B]
            ys.append(y + dsk_ref[:, hd * HEADDIM_B:(hd + 1) * HEADDIM_B] * x)
            b_last = b_c[L - 1:L, :]
            xws.append((jnp.exp(b_last - b_c) * dt_c) * x)
        upd = _mm_tn(jnp.concatenate(xws, axis=1), bm)
        for j in range(HPG_B):
            hd = g * HPG_B + j
            ln = lane0 + hd
            s_ref[hd] = eb[L - 1:L, ln:ln + 1] * s_ref[hd] + upd[j * HEADDIM_B:(j + 1) * HEADDIM_B, :]
        half = D_B // NG_B
        yg = jnp.concatenate(ys, axis=1) * _silu(z_ref[:, g * half:(g + 1) * half])
        y_ref[:, g * half:(g + 1) * half] = _rms(yg, nw_ref[:, g * half:(g + 1) * half])


def _ssd(xbc, z, gates, conv_w, conv_b, dtb_row, alog_row, dskip_row, norm_w, s0, buf0, L):
    bsz, T, _ = xbc.shape
    row128 = _full_spec((1, LANES))
    return pl.pallas_call(
        functools.partial(_ssd_kernel, L=L),
        grid=(bsz, T // L),
        in_specs=[_seq_spec(L, CONV_DIM_B), _seq_spec(L, D_B), _seq_spec(L, LANES),
                  _full_spec((CONV_W, CONV_DIM_B)), _full_spec((1, CONV_DIM_B)), row128, row128,
                  _full_spec((1, D_B)), _full_spec((1, D_B)),
                  _state_spec(s0.shape), _state_spec(buf0.shape)],
        out_specs=[_seq_spec(L, D_B), _state_spec(s0.shape), _state_spec(buf0.shape)],
        out_shape=[jax.ShapeDtypeStruct((bsz, T, D_B), _F32), jax.ShapeDtypeStruct(s0.shape, _F32),
                   jax.ShapeDtypeStruct(buf0.shape, _F32)],
        scratch_shapes=[pltpu.VMEM((CONV_PAD + L, CONV_DIM_B), _F32)],
        compiler_params=_params(("parallel", "arbitrary")),
    )(xbc, z, gates, conv_w, conv_b, dtb_row, alog_row, dskip_row, norm_w, s0, buf0)


def _unit_lower_inverse(a, L):
    r, c = _iota2((L, L), 0), _iota2((L, L), 1)
    eye = (r == c).astype(_F32)
    blk = min(INV_BLOCK, L)
    same = (r // blk) == (c // blk)
    d = jnp.where(same, a, 0.0)
    x = eye - d
    p = d
    steps = int(math.log2(blk)) - 1
    for _ in range(steps):
        p = _mm_hi(p, p)
        x = x + _mm_hi(x, p)
    size = blk
    while size < L:
        off = jnp.where(((r // (2 * size)) == (c // (2 * size))) & ((r // size) != (c // size)), a, 0.0)
        x = x - _mm_hi(_mm_hi(x, off), x)
        size *= 2
    return x


def _gdn_kernel(x_ref, z_ref, g_ref, cw_ref, dtb_ref, alog_ref, nw_ref, s0_ref, buf0_ref,
                o_ref, s_ref, tail_ref, win_ref, *, L):
    @pl.when(pl.program_id(1) == 0)
    def _():
        s_ref[...] = s0_ref[...]

    act = _conv_silu(x_ref, cw_ref, None, buf0_ref, win_ref, tail_ref, L)
    gates = g_ref[...]
    beta = _sigmoid(gates)
    gam = _cumsum_rows(-jnp.exp(alog_ref[...]) * _softplus(gates + dtb_ref[...]))
    egam = jnp.exp(gam)
    gam_t = gam.T
    r, c = _iota2((L, L), 0), _iota2((L, L), 1)
    for h in range(NH_C):
        ln = NH_C + h
        q = act[:, h * DK_C:(h + 1) * DK_C]
        k = act[:, D_CK + h * DK_C:D_CK + (h + 1) * DK_C]
        v = act[:, 2 * D_CK + h * DV_C:2 * D_CK + (h + 1) * DV_C]
        q = q * lax.rsqrt(jnp.sum(q * q, axis=1, keepdims=True) + EPS) * (DK_C ** -0.5)
        k = k * lax.rsqrt(jnp.sum(k * k, axis=1, keepdims=True) + EPS)
        g_c = gam[:, ln:ln + 1]
        eg_c = egam[:, ln:ln + 1]
        beta_c = beta[:, h:h + 1]
        dec = jnp.exp(jnp.where(r >= c, g_c - gam_t[ln:ln + 1, :], -jnp.inf))
        a_mat = jnp.where(r > c, beta_c * dec * _mm_nt(k, k), 0.0)
        t_inv = _unit_lower_inverse(a_mat, L)
        sol_v = _mm_hi(t_inv, beta_c * v)
        sol_k = _mm_hi(t_inv, (beta_c * eg_c) * k)
        s = s_ref[h]
        u = sol_v - _mm(sol_k, s)
        o = eg_c * _mm(q, s) + _mm(_mm_nt(q, k) * dec, u)
        g_last = g_c[L - 1:L, :]
        s_ref[h] = jnp.exp(g_last) * s + _mm_tn(jnp.exp(g_last - g_c) * k, u)
        o_ref[:, h * DV_C:(h + 1) * DV_C] = _rms(o, nw_ref[...]) * _silu(z_ref[:, h * DV_C:(h + 1) * DV_C])


def _gdn(qkv, z, gates, conv_w, dtb_row, alog_row, norm_w, s0, buf0, L):
    bsz, T, _ = qkv.shape
    row128 = _full_spec((1, LANES))
    return pl.pallas_call(
        functools.partial(_gdn_kernel, L=L),
        grid=(bsz, T // L),
        in_specs=[_seq_spec(L, CONV_DIM_C), _seq_spec(L, D_CV), _seq_spec(L, LANES),
                  _full_spec((CONV_W, CONV_DIM_C)), row128, row128, _full_spec((1, DV_C)),
                  _state_spec(s0.shape), _state_spec(buf0.shape)],
        out_specs=[_seq_spec(L, D_CV), _state_spec(s0.shape), _state_spec(buf0.shape)],
        out_shape=[jax.ShapeDtypeStruct((bsz, T, D_CV), _F32), jax.ShapeDtypeStruct(s0.shape, _F32),
                   jax.ShapeDtypeStruct(buf0.shape, _F32)],
        scratch_shapes=[pltpu.VMEM((CONV_PAD + L, CONV_DIM_C), _F32)],
        compiler_params=_params(("parallel", "arbitrary")),
    )(qkv, z, gates, conv_w, dtb_row, alog_row, norm_w, s0, buf0)


def _lane_row(vals, offset):
    return jnp.zeros((1, LANES), _F32).at[0, offset:offset + vals.shape[0]].set(vals.astype(_F32))


def _split(w, sizes):
    out, off = [], 0
    for s in sizes:
        out.append(w[:, off:off + s])
        off += s
    return out


def _gate_cols(cols):
    w = jnp.concatenate(cols, axis=1)
    return jnp.pad(w, ((0, 0), (0, LANES - w.shape[1])))


def _prepare(norm_g, norm_f, ffn_w_gate, ffn_w_up, ffn_w_down, ab_w_in, ab_w_out, mlstm_b_i, mlstm_b_f,
             mlstm_norm, ssd_conv_w, ssd_conv_b, ssd_dt_bias, ssd_a_log, ssd_d, ssd_norm, gdn_w_in, gdn_w_out,
             gdn_conv_w, gdn_dt_bias, gdn_a_log, gdn_norm):
    lo = lambda w: w.astype(_MXU_DTYPE)
    wq, wk, wv, wo, wig, wfg, wz, wxbc, wdt = _split(ab_w_in, SIZES_AB)
    wqkv, wzc, wb, wa = _split(gdn_w_in, SIZES_C)
    return dict(
        norm_g=norm_g, norm_f=norm_f,
        wg=lo(ffn_w_gate), wu=lo(ffn_w_up), wd=lo(ffn_w_down),
        ab_in=[lo(w) for w in (wq, wk, wv, wo, wz, wxbc, _gate_cols([wig, wfg, wdt]))],
        ab_out=[lo(ab_w_out[:D_A]), lo(ab_w_out[D_A:])],
        bi_row=_lane_row(mlstm_b_i, 0), bf_row=_lane_row(mlstm_b_f, NH_A),
        mlstm_norm=mlstm_norm.astype(_F32),
        ssd_conv_w=ssd_conv_w, ssd_conv_b=ssd_conv_b.reshape(1, CONV_DIM_B),
        ssd_dtb=_lane_row(ssd_dt_bias, 2 * NH_A), ssd_alog=_lane_row(ssd_a_log, 2 * NH_A),
        ssd_dskip=jnp.repeat(ssd_d.astype(_F32), HEADDIM_B).reshape(1, D_B),
        ssd_norm=ssd_norm.reshape(1, D_B),
        c_in=[lo(w) for w in (wqkv, wzc, _gate_cols([wb, wa]))],
        c_out=lo(gdn_w_out),
        gdn_conv_w=gdn_conv_w,
        gdn_dtb=_lane_row(gdn_dt_bias, NH_C), gdn_alog=_lane_row(gdn_a_log, NH_C),
        gdn_norm=gdn_norm.reshape(1, DV_C),
    )


def _trunk(x, ab_state, c_state, p):
    bsz, T, _ = x.shape
    L = min(T, 64)
    m = bsz * T
    flat = lambda a: a.reshape(m, a.shape[-1])
    seq = lambda a: a.reshape(bsz, T, a.shape[-1])
    c0, n0, m0, s0, bufb0 = ab_state
    sc0, bufc0 = c_state
    x = flat(x)
    x = _ffn(x, p["norm_g"][0, 0], p["wg"][0, 0], p["wu"][0, 0], p["wd"][0, 0])
    q, k, v, o, z, xbc, gates = _norm_proj(x, p["norm_g"][0, 1], p["ab_in"])
    h_a, c1, n1, m1 = _mlstm(seq(q), seq(k), seq(v), seq(o), seq(gates), p["bi_row"], p["bf_row"],
                             p["mlstm_norm"], c0, n0, m0, L)
    y_b, s1, bufb1 = _ssd(seq(xbc), seq(z), seq(gates), p["ssd_conv_w"], p["ssd_conv_b"], p["ssd_dtb"],
                          p["ssd_alog"], p["ssd_dskip"], p["ssd_norm"], s0, bufb0, L)
    x = _out_proj(x, [flat(h_a), flat(y_b)], p["ab_out"])
    x = _ffn(x, p["norm_g"][0, 2], p["wg"][0, 1], p["wu"][0, 1], p["wd"][0, 1])
    x = _ffn(x, p["norm_g"][1, 0], p["wg"][1, 0], p["wu"][1, 0], p["wd"][1, 0])
    qkv, zc, gates_c = _norm_proj(x, p["norm_g"][1, 1], p["c_in"])
    o_c, sc1, bufc1 = _gdn(seq(qkv), seq(zc), seq(gates_c), p["gdn_conv_w"], p["gdn_dtb"], p["gdn_alog"],
                           p["gdn_norm"], sc0, bufc0, L)
    x = _out_proj(x, [flat(o_c)], [p["c_out"]])
    y = _ffn(x, p["norm_g"][1, 2], p["wg"][1, 1], p["wu"][1, 1], p["wd"][1, 1], final_g=p["norm_f"])
    return seq(y), (c1, n1, m1, s1, bufb1), (sc1, bufc1)


def kernel(x_prompt, x_sample, state_mlstm_C, state_mlstm_n, state_mlstm_m, state_ssd, cache_ssd_conv, state_gdn, cache_gdn_conv, norm_g, norm_f, ffn_w_gate, ffn_w_up, ffn_w_down, ab_w_in, ab_w_out, mlstm_b_i, mlstm_b_f, mlstm_norm, ssd_conv_w, ssd_conv_b, ssd_dt_bias, ssd_a_log, ssd_d, ssd_norm, gdn_w_in, gdn_w_out, gdn_conv_w, gdn_dt_bias, gdn_a_log, gdn_norm):
    p = _prepare(norm_g, norm_f, ffn_w_gate, ffn_w_up, ffn_w_down, ab_w_in, ab_w_out, mlstm_b_i, mlstm_b_f,
                 mlstm_norm, ssd_conv_w, ssd_conv_b, ssd_dt_bias, ssd_a_log, ssd_d, ssd_norm, gdn_w_in,
                 gdn_w_out, gdn_conv_w, gdn_dt_bias, gdn_a_log, gdn_norm)
    bp = x_prompt.shape[0]
    zeros = lambda *s: jnp.zeros(s, _F32)
    ab0 = (zeros(bp, NH_A, DK_A, DV_A), zeros(bp, NH_A, DK_A), zeros(bp, NH_A),
           zeros(bp, NH_B, HEADDIM_B, D_STATE), zeros(bp, CONV_W - 1, CONV_DIM_B))
    c0 = (zeros(bp, NH_C, DK_C, DV_C), zeros(bp, CONV_W - 1, CONV_DIM_C))
    y_p, ab_p, c_p = _trunk(x_prompt, ab0, c0, p)
    y_s, ab_s, c_s = _trunk(x_sample, (state_mlstm_C, state_mlstm_n, state_mlstm_m, state_ssd, cache_ssd_conv),
                            (state_gdn, cache_gdn_conv), p)
    return (y_p, y_s, *ab_p, *c_p, *ab_s, *c_s)
```

```python
import functools
import math

import jax
import jax.numpy as jnp
from jax import lax
from jax.experimental import pallas as pl
from jax.experimental.pallas import tpu as pltpu

D_MODEL = 1024
EPS = 1e-6
CONV_W = 4
NH_A, DK_A, DV_A = 4, 128, 256
D_A = NH_A * DV_A
NH_B, HEADDIM_B, D_STATE, NG_B = 16, 64, 128, 2
D_B = NH_B * HEADDIM_B
HPG_B = NH_B // NG_B
CONV_DIM_B = D_B + 2 * NG_B * D_STATE
NH_C, DK_C, DV_C = 8, 128, 128
D_CK = NH_C * DK_C
D_CV = NH_C * DV_C
CONV_DIM_C = 2 * D_CK + D_CV
D_FF = 2816
SIZES_AB = (NH_A * DK_A, NH_A * DK_A, D_A, D_A, NH_A, NH_A, D_B, CONV_DIM_B, NH_B)
SIZES_C = (CONV_DIM_C, D_CV, NH_C, NH_C)

LANES = 128
CONV_PAD = 8
VMEM_LIMIT = 56 * 1024 * 1024
FFN_COLS = 256
INV_BLOCK = 16
CHUNK_MLSTM, CHUNK_SSD, CHUNK_GDN = 256, 128, 64

_MXU_DTYPE = jnp.bfloat16
_F32 = jnp.float32
_HI = lax.Precision.HIGHEST


def _mm(a, b):
    return jnp.dot(a.astype(_MXU_DTYPE), b.astype(_MXU_DTYPE), preferred_element_type=_F32)


def _mm_nt(a, b):
    return lax.dot_general(a.astype(_MXU_DTYPE), b.astype(_MXU_DTYPE),
                           (((1,), (1,)), ((), ())), preferred_element_type=_F32)


def _mm_tn(a, b):
    return lax.dot_general(a.astype(_MXU_DTYPE), b.astype(_MXU_DTYPE),
                           (((0,), (0,)), ((), ())), preferred_element_type=_F32)


def _mm_hi(a, b):
    return jnp.dot(a, b, precision=_HI, preferred_element_type=_F32)


def _rms(x, g):
    return x * lax.rsqrt(jnp.mean(x * x, axis=-1, keepdims=True) + EPS) * g


def _sigmoid(x):
    return 1.0 / (1.0 + jnp.exp(-x))


def _silu(x):
    return x * _sigmoid(x)


def _softplus(x):
    return jnp.maximum(x, 0.0) + jnp.log1p(jnp.exp(-jnp.abs(x)))


def _log_sigmoid(x):
    return -_softplus(-x)


def _iota2(shape, dim):
    return lax.broadcasted_iota(jnp.int32, shape, dim)


def _cumsum_rows(x):
    n = x.shape[0]
    tri = (_iota2((n, n), 0) >= _iota2((n, n), 1)).astype(_F32)
    return _mm_hi(tri, x)


def _full_spec(shape):
    nd = len(shape)
    return pl.BlockSpec(shape, lambda *_: (0,) * nd)


def _row_tile(m, cap):
    t = min(m, cap)
    assert m % t == 0
    return t


def _params(sem):
    return pltpu.CompilerParams(dimension_semantics=sem, vmem_limit_bytes=VMEM_LIMIT)


def _ffn_kernel(x_ref, g_ref, wg_ref, wu_ref, wd_ref, *rest, final_norm):
    if final_norm:
        gf_ref, o_ref, h_ref = rest
    else:
        o_ref, h_ref = rest
    x = x_ref[...]
    hn = _rms(x, g_ref[...]).astype(_MXU_DTYPE)
    for c in range(D_FF // FFN_COLS):
        cols = slice(c * FFN_COLS, (c + 1) * FFN_COLS)
        gate = jnp.dot(hn, wg_ref[:, cols], preferred_element_type=_F32)
        up = jnp.dot(hn, wu_ref[:, cols], preferred_element_type=_F32)
        h_ref[:, cols] = (_silu(gate) * up).astype(_MXU_DTYPE)
    y = x + 0.5 * jnp.dot(h_ref[...], wd_ref[...], preferred_element_type=_F32)
    if final_norm:
        y = _rms(y, gf_ref[...])
    o_ref[...] = y


def _ffn(x, g, wg, wu, wd, layer, slot, final_g=None):
    m = x.shape[0]
    tm = _row_tile(m, 512)
    row = pl.BlockSpec((tm, D_MODEL), lambda i: (i, 0))
    pick = lambda w: pl.BlockSpec((None, None) + w.shape[2:], lambda i: (layer, slot, 0, 0))
    ins = [x, g.reshape(1, D_MODEL), wg, wu, wd]
    specs = [row, _full_spec((1, D_MODEL)), pick(wg), pick(wu), pick(wd)]
    if final_g is not None:
        ins.append(final_g.reshape(1, D_MODEL))
        specs.append(_full_spec((1, D_MODEL)))
    return pl.pallas_call(
        functools.partial(_ffn_kernel, final_norm=final_g is not None),
        grid=(m // tm,),
        in_specs=specs,
        out_specs=row,
        out_shape=jax.ShapeDtypeStruct((m, D_MODEL), _F32),
        scratch_shapes=[pltpu.VMEM((tm, D_FF), _MXU_DTYPE)],
        compiler_params=_params(("parallel",)),
    )(*ins)


def _norm_proj_kernel(x_ref, g_ref, *refs, n_out):
    w_refs, o_refs = refs[:n_out], refs[n_out:]
    hn = _rms(x_ref[...], g_ref[...]).astype(_MXU_DTYPE)
    for w_ref, o_ref in zip(w_refs, o_refs):
        o_ref[...] = jnp.dot(hn, w_ref[...], preferred_element_type=_F32)


def _norm_proj(x, g, ws):
    m = x.shape[0]
    tm = _row_tile(m, 256)
    row = lambda n: pl.BlockSpec((tm, n), lambda i: (i, 0))
    return pl.pallas_call(
        functools.partial(_norm_proj_kernel, n_out=len(ws)),
        grid=(m // tm,),
        in_specs=[row(D_MODEL), _full_spec((1, D_MODEL))] + [_full_spec(w.shape) for w in ws],
        out_specs=[row(w.shape[1]) for w in ws],
        out_shape=[jax.ShapeDtypeStruct((m, w.shape[1]), _F32) for w in ws],
        compiler_params=_params(("parallel",)),
    )(x, g.reshape(1, D_MODEL), *ws)


def _out_proj_kernel(x_ref, *refs, n_in):
    a_refs, w_refs, o_ref = refs[:n_in], refs[n_in:2 * n_in], refs[2 * n_in]
    y = x_ref[...]
    for a_ref, w_ref in zip(a_refs, w_refs):
        y = y + jnp.dot(a_ref[...].astype(_MXU_DTYPE), w_ref[...], preferred_element_type=_F32)
    o_ref[...] = y


def _out_proj(x, acts, ws):
    m = x.shape[0]
    tm = _row_tile(m, 512)
    row = lambda n: pl.BlockSpec((tm, n), lambda i: (i, 0))
    return pl.pallas_call(
        functools.partial(_out_proj_kernel, n_in=len(acts)),
        grid=(m // tm,),
        in_specs=[row(D_MODEL)] + [row(a.shape[1]) for a in acts] + [_full_spec(w.shape) for w in ws],
        out_specs=row(D_MODEL),
        out_shape=jax.ShapeDtypeStruct((m, D_MODEL), _F32),
        compiler_params=_params(("parallel",)),
    )(x, *acts, *ws)


def _seq_spec(L, n):
    return pl.BlockSpec((None, L, n), lambda b, c: (b, c, 0))


def _state_spec(shape):
    nd = len(shape)
    return pl.BlockSpec((None,) + tuple(shape[1:]), lambda b, c: (b,) + (0,) * (nd - 1))


def _conv_silu(x_ref, w_ref, bias, buf0_ref, win_ref, tail_ref, L):
    lo = CONV_PAD - (CONV_W - 1)

    @pl.when(pl.program_id(1) == 0)
    def _():
        win_ref[lo:CONV_PAD, :] = buf0_ref[...]

    win_ref[CONV_PAD:CONV_PAD + L, :] = x_ref[...]
    y = w_ref[0:1, :] * win_ref[lo:lo + L, :]
    for i in range(1, CONV_W):
        y = y + w_ref[i:i + 1, :] * win_ref[lo + i:lo + i + L, :]
    if bias is not None:
        y = y + bias
    tail = win_ref[L + lo:L + CONV_PAD, :]
    win_ref[lo:CONV_PAD, :] = tail
    tail_ref[...] = tail
    return _silu(y)


def _mlstm_kernel(q_ref, k_ref, v_ref, o_ref, g_ref, bi_ref, bf_ref, nw_ref, c0_ref, n0_ref, m0_ref,
                  h_ref, c_ref, n_ref, m_ref, *, L):
    @pl.when(pl.program_id(1) == 0)
    def _():
        c_ref[...] = c0_ref[...]
        n_ref[...] = n0_ref[...]
        m_ref[...] = m0_ref[...]

    gates = g_ref[...]
    ig = gates + bi_ref[...]
    b = _cumsum_rows(_log_sigmoid(gates + bf_ref[...]))
    ig_t, b_t = ig.T, b.T
    causal = _iota2((L, L), 0) >= _iota2((L, L), 1)
    lane = _iota2((1, NH_A), 1)
    m_old = m_ref[...]
    m_next = m_old
    for h in range(NH_A):
        q = q_ref[:, h * DK_A:(h + 1) * DK_A]
        k = k_ref[:, h * DK_A:(h + 1) * DK_A] * (DK_A ** -0.5)
        v = v_ref[:, h * DV_A:(h + 1) * DV_A]
        b_c = b[:, NH_A + h:NH_A + h + 1]
        ig_c = ig[:, h:h + 1]
        row = ig_t[h:h + 1, :] - b_t[NH_A + h:NH_A + h + 1, :]
        m_prev = m_old[:, h:h + 1]
        dmat = jnp.where(causal, b_c + row, -jnp.inf)
        inter = b_c + m_prev
        m_s = jnp.maximum(inter, jnp.max(dmat, axis=1, keepdims=True))
        scores = _mm_nt(q, k) * jnp.exp(dmat - m_s)
        w_inter = jnp.exp(inter - m_s)
        num = _mm(scores, v) + w_inter * _mm(q, c_ref[h])
        den = (jnp.sum(scores, axis=1, keepdims=True)
               + w_inter * jnp.sum(q * n_ref[h:h + 1, :], axis=1, keepdims=True))
        hh = num / jnp.maximum(jnp.abs(den), jnp.exp(-m_s))
        hh = _rms(hh, nw_ref[h:h + 1, :])
        h_ref[:, h * DV_A:(h + 1) * DV_A] = _sigmoid(o_ref[:, h * DV_A:(h + 1) * DV_A]) * hh
        b_last = b_c[L - 1:L, :]
        dl = b_last - b_c + ig_c
        m_new = jnp.maximum(b_last + m_prev, jnp.max(dl, axis=0, keepdims=True))
        kw = k * jnp.exp(dl - m_new)
        decay = jnp.exp(b_last + m_prev - m_new)
        c_ref[h] = decay * c_ref[h] + _mm_tn(kw, v)
        n_ref[h:h + 1, :] = decay * n_ref[h:h + 1, :] + jnp.sum(kw, axis=0, keepdims=True)
        m_next = jnp.where(lane == h, m_new, m_next)
    m_ref[...] = m_next


def _mlstm(q, k, v, o, gates, bi_row, bf_row, norm_w, c0, n0, m0, L):
    bsz, T, _ = q.shape
    m0 = m0.reshape(bsz, 1, NH_A)
    row128 = _full_spec((1, LANES))
    h, c1, n1, m1 = pl.pallas_call(
        functools.partial(_mlstm_kernel, L=L),
        grid=(bsz, T // L),
        in_specs=[_seq_spec(L, NH_A * DK_A), _seq_spec(L, NH_A * DK_A), _seq_spec(L, D_A), _seq_spec(L, D_A),
                  _seq_spec(L, LANES), row128, row128, _full_spec((NH_A, DV_A)),
                  _state_spec(c0.shape), _state_spec(n0.shape), _state_spec(m0.shape)],
        out_specs=[_seq_spec(L, D_A), _state_spec(c0.shape), _state_spec(n0.shape), _state_spec(m0.shape)],
        out_shape=[jax.ShapeDtypeStruct((bsz, T, D_A), _F32),
                   jax.ShapeDtypeStruct(c0.shape, _F32), jax.ShapeDtypeStruct(n0.shape, _F32),
                   jax.ShapeDtypeStruct(m0.shape, _F32)],
        compiler_params=_params(("parallel", "arbitrary")),
    )(q, k, v, o, gates, bi_row, bf_row, norm_w, c0, n0, m0)
    return h, c1, n1, m1.reshape(bsz, NH_A)


def _ssd_kernel(x_ref, z_ref, g_ref, cw_ref, cb_ref, dtb_ref, alog_ref, dsk_ref, nw_ref, s0_ref, buf0_ref,
                y_ref, s_ref, tail_ref, win_ref, *, L):
    @pl.when(pl.program_id(1) == 0)
    def _():
        s_ref[...] = s0_ref[...]

    act = _conv_silu(x_ref, cw_ref, cb_ref[...], buf0_ref, win_ref, tail_ref, L)
    dt = _softplus(g_ref[...] + dtb_ref[...])
    b = _cumsum_rows(-jnp.exp(alog_ref[...]) * dt)
    eb = jnp.exp(b)
    b_t = b.T
    causal = _iota2((L, L), 0) >= _iota2((L, L), 1)
    lane0 = 2 * NH_A
    for g in range(NG_B):
        bm = act[:, D_B + g * D_STATE:D_B + (g + 1) * D_STATE]
        cm = act[:, D_B + (NG_B + g) * D_STATE:D_B + (NG_B + g + 1) * D_STATE]
        cb = _mm_nt(cm, bm)
        s_g = s_ref[g * HPG_B:(g + 1) * HPG_B].reshape(HPG_B * HEADDIM_B, D_STATE)
        y_inter = _mm_nt(cm, s_g)
        ys, xws = [], []
        for j in range(HPG_B):
            hd = g * HPG_B + j
            ln = lane0 + hd
            x = act[:, hd * HEADDIM_B:(hd + 1) * HEADDIM_B]
            b_c = b[:, ln:ln + 1]
            dt_c = dt[:, ln:ln + 1]
            seg = jnp.exp(jnp.where(causal, b_c - b_t[ln:ln + 1, :], -jnp.inf))
            y = _mm(cb * seg, dt_c * x) + eb[:, ln:ln + 1] * y_inter[:, j * HEADDIM_B:(j + 1) * HEADDIM_B]
            ys.append(y + dsk_ref[:, hd * HEADDIM_B:(hd + 1) * HEADDIM_B] * x)
            b_last = b_c[L - 1:L, :]
            xws.append((jnp.exp(b_last - b_c) * dt_c) * x)
        upd = _mm_tn(jnp.concatenate(xws, axis=1), bm)
        for j in range(HPG_B):
            hd = g * HPG_B + j
            ln = lane0 + hd
            s_ref[hd] = eb[L - 1:L, ln:ln + 1] * s_ref[hd] + upd[j * HEADDIM_B:(j + 1) * HEADDIM_B, :]
        half = D_B // NG_B
        yg = jnp.concatenate(ys, axis=1) * _silu(z_ref[:, g * half:(g + 1) * half])
        y_ref[:, g * half:(g + 1) * half] = _rms(yg, nw_ref[:, g * half:(g + 1) * half])


def _ssd(xbc, z, gates, conv_w, conv_b, dtb_row, alog_row, dskip_row, norm_w, s0, buf0, L):
    bsz, T, _ = xbc.shape
    row128 = _full_spec((1, LANES))
    return pl.pallas_call(
        functools.partial(_ssd_kernel, L=L),
        grid=(bsz, T // L),
        in_specs=[_seq_spec(L, CONV_DIM_B), _seq_spec(L, D_B), _seq_spec(L, LANES),
                  _full_spec((CONV_W, CONV_DIM_B)), _full_spec((1, CONV_DIM_B)), row128, row128,
                  _full_spec((1, D_B)), _full_spec((1, D_B)),
                  _state_spec(s0.shape), _state_spec(buf0.shape)],
        out_specs=[_seq_spec(L, D_B), _state_spec(s0.shape), _state_spec(buf0.shape)],
        out_shape=[jax.ShapeDtypeStruct((bsz, T, D_B), _F32), jax.ShapeDtypeStruct(s0.shape, _F32),
                   jax.ShapeDtypeStruct(buf0.shape, _F32)],
        scratch_shapes=[pltpu.VMEM((CONV_PAD + L, CONV_DIM_B), _F32)],
        compiler_params=_params(("parallel", "arbitrary")),
    )(xbc, z, gates, conv_w, conv_b, dtb_row, alog_row, dskip_row, norm_w, s0, buf0)


def _split_lo(a):
    hi = a.astype(_MXU_DTYPE)
    lo = (a - hi.astype(_F32)).astype(_MXU_DTYPE)
    return hi, lo


def _bmm(a, b):
    return jnp.einsum("hij,hjk->hik", a.astype(_MXU_DTYPE), b.astype(_MXU_DTYPE), preferred_element_type=_F32)


def _bmm_nt(a, b):
    return jnp.einsum("hik,hjk->hij", a.astype(_MXU_DTYPE), b.astype(_MXU_DTYPE), preferred_element_type=_F32)


def _bmm_split(a, b):
    (ah, al), (bh, bl) = a, b
    return _bmm(ah, bh) + _bmm(ah, bl) + _bmm(al, bh)


def _unit_lower_inverse(a, L):
    r, c = _iota2((1, L, L), 1), _iota2((1, L, L), 2)
    eye = (r == c).astype(_F32)
    blk = min(INV_BLOCK, L)
    shift = int(math.log2(blk))
    d = jnp.where((r >> shift) == (c >> shift), a, 0.0)
    x = eye - d
    p = _split_lo(d)
    for _ in range(shift - 1):
        p = _split_lo(_bmm_split(p, p))
        x = x + _bmm_split(_split_lo(x), p)
    while (1 << shift) < L:
        off = jnp.where(((r >> (shift + 1)) == (c >> (shift + 1))) & ((r >> shift) != (c >> shift)), a, 0.0)
        xs = _split_lo(x)
        x = x - _bmm_split(_split_lo(_bmm_split(xs, _split_lo(off))), xs)
        shift += 1
    return x


def _gdn_kernel(x_ref, z_ref, g_ref, cw_ref, dtb_ref, alog_ref, nw_ref, s0_ref, buf0_ref,
                o_ref, s_ref, tail_ref, win_ref, *, L):
    @pl.when(pl.program_id(1) == 0)
    def _():
        s_ref[...] = s0_ref[...]

    act = _conv_silu(x_ref, cw_ref, None, buf0_ref, win_ref, tail_ref, L)
    heads = lambda a, off, w: jnp.stack([a[:, off + h * w:off + (h + 1) * w] for h in range(NH_C)])
    q, k, v = heads(act, 0, DK_C), heads(act, D_CK, DK_C), heads(act, 2 * D_CK, DV_C)
    q = q * (lax.rsqrt(jnp.sum(q * q, axis=-1, keepdims=True) + EPS) * (DK_C ** -0.5))
    k = k * lax.rsqrt(jnp.sum(k * k, axis=-1, keepdims=True) + EPS)
    gates = g_ref[...]
    beta = _sigmoid(gates)
    gam = _cumsum_rows(-jnp.exp(alog_ref[...]) * _softplus(gates + dtb_ref[...]))
    gam_t = gam.T
    g_c = heads(gam, NH_C, 1)
    g_r = jnp.stack([gam_t[NH_C + h:NH_C + h + 1, :] for h in range(NH_C)])
    beta_c = heads(beta, 0, 1)
    eg_c = jnp.exp(g_c)
    r, c = _iota2((1, L, L), 1), _iota2((1, L, L), 2)
    dec = jnp.exp(jnp.where(r >= c, g_c - g_r, -jnp.inf))
    kb = k.astype(_MXU_DTYPE)
    a_mat = jnp.where(r > c, beta_c * dec * _bmm_nt(kb, kb), 0.0)
    th, tl = _split_lo(_unit_lower_inverse(a_mat, L))
    rhs = jnp.concatenate([beta_c * v, (beta_c * eg_c) * k], axis=-1).astype(_MXU_DTYPE)
    sol = _bmm(th, rhs) + _bmm(tl, rhs)
    s = s_ref[...]
    sb = s.astype(_MXU_DTYPE)
    u = sol[:, :, :DV_C] - _bmm(sol[:, :, DV_C:], sb)
    o = eg_c * _bmm(q, sb) + _bmm(_bmm_nt(q, kb) * dec, u)
    g_last = g_c[:, L - 1:L, :]
    kw = jnp.exp(g_last - g_c) * k
    for h in range(NH_C):
        s_ref[h] = jnp.exp(g_last[h]) * s[h] + _mm_tn(kw[h], u[h])
    o = _rms(o, nw_ref[...]) * _silu(heads(z_ref[...], 0, DV_C))
    for h in range(NH_C):
        o_ref[:, h * DV_C:(h + 1) * DV_C] = o[h]


def _gdn(qkv, z, gates, conv_w, dtb_row, alog_row, norm_w, s0, buf0, L):
    bsz, T, _ = qkv.shape
    row128 = _full_spec((1, LANES))
    return pl.pallas_call(
        functools.partial(_gdn_kernel, L=L),
        grid=(bsz, T // L),
        in_specs=[_seq_spec(L, CONV_DIM_C), _seq_spec(L, D_CV), _seq_spec(L, LANES),
                  _full_spec((CONV_W, CONV_DIM_C)), row128, row128, _full_spec((1, DV_C)),
                  _state_spec(s0.shape), _state_spec(buf0.shape)],
        out_specs=[_seq_spec(L, D_CV), _state_spec(s0.shape), _state_spec(buf0.shape)],
        out_shape=[jax.ShapeDtypeStruct((bsz, T, D_CV), _F32), jax.ShapeDtypeStruct(s0.shape, _F32),
                   jax.ShapeDtypeStruct(buf0.shape, _F32)],
        scratch_shapes=[pltpu.VMEM((CONV_PAD + L, CONV_DIM_C), _F32)],
        compiler_params=_params(("parallel", "arbitrary")),
    )(qkv, z, gates, conv_w, dtb_row, alog_row, norm_w, s0, buf0)


def _lane_row(vals, offset):
    return jnp.zeros((1, LANES), _F32).at[0, offset:offset + vals.shape[0]].set(vals.astype(_F32))


def _split(w, sizes):
    out, off = [], 0
    for s in sizes:
        out.append(w[:, off:off + s])
        off += s
    return out


def _gate_cols(cols):
    w = jnp.concatenate(cols, axis=1)
    return jnp.pad(w, ((0, 0), (0, LANES - w.shape[1])))


def _prepare(norm_g, norm_f, ffn_w_gate, ffn_w_up, ffn_w_down, ab_w_in, ab_w_out, mlstm_b_i, mlstm_b_f,
             mlstm_norm, ssd_conv_w, ssd_conv_b, ssd_dt_bias, ssd_a_log, ssd_d, ssd_norm, gdn_w_in, gdn_w_out,
             gdn_conv_w, gdn_dt_bias, gdn_a_log, gdn_norm):
    lo = lambda w: w.astype(_MXU_DTYPE)
    wq, wk, wv, wo, wig, wfg, wz, wxbc, wdt = _split(ab_w_in, SIZES_AB)
    wqkv, wzc, wb, wa = _split(gdn_w_in, SIZES_C)
    return dict(
        norm_g=norm_g, norm_f=norm_f,
        wg=lo(ffn_w_gate), wu=lo(ffn_w_up), wd=lo(ffn_w_down),
        ab_in=[lo(w) for w in (wq, wk, wv, wo, wz, wxbc, _gate_cols([wig, wfg, wdt]))],
        ab_out=[lo(ab_w_out[:D_A]), lo(ab_w_out[D_A:])],
        bi_row=_lane_row(mlstm_b_i, 0), bf_row=_lane_row(mlstm_b_f, NH_A),
        mlstm_norm=mlstm_norm.astype(_F32),
        ssd_conv_w=ssd_conv_w, ssd_conv_b=ssd_conv_b.reshape(1, CONV_DIM_B),
        ssd_dtb=_lane_row(ssd_dt_bias, 2 * NH_A), ssd_alog=_lane_row(ssd_a_log, 2 * NH_A),
        ssd_dskip=jnp.repeat(ssd_d.astype(_F32), HEADDIM_B).reshape(1, D_B),
        ssd_norm=ssd_norm.reshape(1, D_B),
        c_in=[lo(w) for w in (wqkv, wzc, _gate_cols([wb, wa]))],
        c_out=lo(gdn_w_out),
        gdn_conv_w=gdn_conv_w,
        gdn_dtb=_lane_row(gdn_dt_bias, NH_C), gdn_alog=_lane_row(gdn_a_log, NH_C),
        gdn_norm=gdn_norm.reshape(1, DV_C),
    )


def _trunk(x, ab_state, c_state, p):
    bsz, T, _ = x.shape
    la, lb, lc = min(T, CHUNK_MLSTM), min(T, CHUNK_SSD), min(T, CHUNK_GDN)
    m = bsz * T
    flat = lambda a: a.reshape(m, a.shape[-1])
    seq = lambda a: a.reshape(bsz, T, a.shape[-1])
    c0, n0, m0, s0, bufb0 = ab_state
    sc0, bufc0 = c_state
    x = flat(x)
    ffn = lambda x, l, i, **kw: _ffn(x, p["norm_g"][l, 2 * i], p["wg"], p["wu"], p["wd"], l, i, **kw)
    x = ffn(x, 0, 0)
    q, k, v, o, z, xbc, gates = _norm_proj(x, p["norm_g"][0, 1], p["ab_in"])
    h_a, c1, n1, m1 = _mlstm(seq(q), seq(k), seq(v), seq(o), seq(gates), p["bi_row"], p["bf_row"],
                             p["mlstm_norm"], c0, n0, m0, la)
    y_b, s1, bufb1 = _ssd(seq(xbc), seq(z), seq(gates), p["ssd_conv_w"], p["ssd_conv_b"], p["ssd_dtb"],
                          p["ssd_alog"], p["ssd_dskip"], p["ssd_norm"], s0, bufb0, lb)
    x = _out_proj(x, [flat(h_a), flat(y_b)], p["ab_out"])
    x = ffn(x, 0, 1)
    x = ffn(x, 1, 0)
    qkv, zc, gates_c = _norm_proj(x, p["norm_g"][1, 1], p["c_in"])
    o_c, sc1, bufc1 = _gdn(seq(qkv), seq(zc), seq(gates_c), p["gdn_conv_w"], p["gdn_dtb"], p["gdn_alog"],
                           p["gdn_norm"], sc0, bufc0, lc)
    x = _out_proj(x, [flat(o_c)], [p["c_out"]])
    y = ffn(x, 1, 1, final_g=p["norm_f"])
    return seq(y), (c1, n1, m1, s1, bufb1), (sc1, bufc1)


def kernel(x_prompt, x_sample, state_mlstm_C, state_mlstm_n, state_mlstm_m, state_ssd, cache_ssd_conv, state_gdn, cache_gdn_conv, norm_g, norm_f, ffn_w_gate, ffn_w_up, ffn_w_down, ab_w_in, ab_w_out, mlstm_b_i, mlstm_b_f, mlstm_norm, ssd_conv_w, ssd_conv_b, ssd_dt_bias, ssd_a_log, ssd_d, ssd_norm, gdn_w_in, gdn_w_out, gdn_conv_w, gdn_dt_bias, gdn_a_log, gdn_norm):
    p = _prepare(norm_g, norm_f, ffn_w_gate, ffn_w_up, ffn_w_down, ab_w_in, ab_w_out, mlstm_b_i, mlstm_b_f,
                 mlstm_norm, ssd_conv_w, ssd_conv_b, ssd_dt_bias, ssd_a_log, ssd_d, ssd_norm, gdn_w_in,
                 gdn_w_out, gdn_conv_w, gdn_dt_bias, gdn_a_log, gdn_norm)
    bp = x_prompt.shape[0]
    zeros = lambda *s: jnp.zeros(s, _F32)
    ab0 = (zeros(bp, NH_A, DK_A, DV_A), zeros(bp, NH_A, DK_A), zeros(bp, NH_A),
           zeros(bp, NH_B, HEADDIM_B, D_STATE), zeros(bp, CONV_W - 1, CONV_DIM_B))
    c0 = (zeros(bp, NH_C, DK_C, DV_C), zeros(bp, CONV_W - 1, CONV_DIM_C))
    y_p, ab_p, c_p = _trunk(x_prompt, ab0, c0, p)
    y_s, ab_s, c_s = _trunk(x_sample, (state_mlstm_C, state_mlstm_n, state_mlstm_m, state_ssd, cache_ssd_conv),
                            (state_gdn, cache_gdn_conv), p)
    return (y_p, y_s, *ab_p, *c_p, *ab_s, *c_s)
```

```python
import functools
import math

import jax
import jax.numpy as jnp
from jax import lax
from jax.experimental import pallas as pl
from jax.experimental.pallas import tpu as pltpu

D_MODEL = 1024
EPS = 1e-6
CONV_W = 4
NH_A, DK_A, DV_A = 4, 128, 256
D_A = NH_A * DV_A
NH_B, HEADDIM_B, D_STATE, NG_B = 16, 64, 128, 2
D_B = NH_B * HEADDIM_B
HPG_B = NH_B // NG_B
CONV_DIM_B = D_B + 2 * NG_B * D_STATE
NH_C, DK_C, DV_C = 8, 128, 128
D_CK = NH_C * DK_C
D_CV = NH_C * DV_C
CONV_DIM_C = 2 * D_CK + D_CV
D_FF = 2816
SIZES_AB = (NH_A * DK_A, NH_A * DK_A, D_A, D_A, NH_A, NH_A, D_B, CONV_DIM_B, NH_B)
SIZES_C = (CONV_DIM_C, D_CV, NH_C, NH_C)

LANES = 128
CONV_PAD = 8
VMEM_LIMIT = 56 * 1024 * 1024
FFN_COLS = 256
INV_BLOCK = 16
CHUNK_MLSTM, CHUNK_SSD, CHUNK_GDN = 256, 128, 64

_MXU_DTYPE = jnp.bfloat16
_F32 = jnp.float32
_HI = lax.Precision.HIGHEST


def _mm(a, b):
    return jnp.dot(a.astype(_MXU_DTYPE), b.astype(_MXU_DTYPE), preferred_element_type=_F32)


def _mm_nt(a, b):
    return lax.dot_general(a.astype(_MXU_DTYPE), b.astype(_MXU_DTYPE),
                           (((1,), (1,)), ((), ())), preferred_element_type=_F32)


def _mm_tn(a, b):
    return lax.dot_general(a.astype(_MXU_DTYPE), b.astype(_MXU_DTYPE),
                           (((0,), (0,)), ((), ())), preferred_element_type=_F32)


def _mm_hi(a, b):
    return jnp.dot(a, b, precision=_HI, preferred_element_type=_F32)


def _rms(x, g):
    return x * lax.rsqrt(jnp.mean(x * x, axis=-1, keepdims=True) + EPS) * g


def _sigmoid(x):
    return 1.0 / (1.0 + jnp.exp(-x))


def _silu(x):
    return x * _sigmoid(x)


def _softplus(x):
    return jnp.maximum(x, 0.0) + jnp.log1p(jnp.exp(-jnp.abs(x)))


def _log_sigmoid(x):
    return -_softplus(-x)


def _iota2(shape, dim):
    return lax.broadcasted_iota(jnp.int32, shape, dim)


def _cumsum_rows(x):
    n = x.shape[0]
    tri = (_iota2((n, n), 0) >= _iota2((n, n), 1)).astype(_F32)
    return _mm_hi(tri, x)


def _full_spec(shape):
    nd = len(shape)
    return pl.BlockSpec(shape, lambda *_: (0,) * nd)


def _row_tile(m, cap):
    t = min(m, cap)
    assert m % t == 0
    return t


def _params(sem):
    return pltpu.CompilerParams(dimension_semantics=sem, vmem_limit_bytes=VMEM_LIMIT)


def _ffn_kernel(x_ref, g_ref, wg_ref, wu_ref, wd_ref, *rest, final_norm, n_pre):
    pre, rest = rest[:2 * n_pre], rest[2 * n_pre:]
    if final_norm:
        gf_ref, o_ref, h_ref = rest
    else:
        o_ref, h_ref = rest
    x = x_ref[...]
    for a_ref, w_ref in zip(pre[:n_pre], pre[n_pre:]):
        x = x + jnp.dot(a_ref[...].astype(_MXU_DTYPE), w_ref[...], preferred_element_type=_F32)
    hn = _rms(x, g_ref[...]).astype(_MXU_DTYPE)
    for c in range(D_FF // FFN_COLS):
        cols = slice(c * FFN_COLS, (c + 1) * FFN_COLS)
        gate = jnp.dot(hn, wg_ref[:, cols], preferred_element_type=_F32)
        up = jnp.dot(hn, wu_ref[:, cols], preferred_element_type=_F32)
        h_ref[:, cols] = (_silu(gate) * up).astype(_MXU_DTYPE)
    y = x + 0.5 * jnp.dot(h_ref[...], wd_ref[...], preferred_element_type=_F32)
    if final_norm:
        y = _rms(y, gf_ref[...])
    o_ref[...] = y


def _ffn(x, g, wg, wu, wd, layer, slot, final_g=None, pre=None):
    m = x.shape[0]
    tm = _row_tile(m, 512)
    row = lambda n: pl.BlockSpec((tm, n), lambda i: (i, 0))
    once = pl.Buffered(1)
    pick = lambda w: pl.BlockSpec((None, None) + w.shape[2:], lambda i: (layer, slot, 0, 0), pipeline_mode=once)
    acts, ws = pre if pre is not None else ([], [])
    ins = [x, g.reshape(1, D_MODEL), wg, wu, wd, *acts, *ws]
    specs = ([row(D_MODEL), _full_spec((1, D_MODEL)), pick(wg), pick(wu), pick(wd)]
             + [row(a.shape[1]) for a in acts]
             + [pl.BlockSpec(w.shape, lambda i: (0, 0), pipeline_mode=once) for w in ws])
    if final_g is not None:
        ins.append(final_g.reshape(1, D_MODEL))
        specs.append(_full_spec((1, D_MODEL)))
    row = row(D_MODEL)
    return pl.pallas_call(
        functools.partial(_ffn_kernel, final_norm=final_g is not None, n_pre=len(acts)),
        grid=(m // tm,),
        in_specs=specs,
        out_specs=row,
        out_shape=jax.ShapeDtypeStruct((m, D_MODEL), _F32),
        scratch_shapes=[pltpu.VMEM((tm, D_FF), _MXU_DTYPE)],
        compiler_params=_params(("parallel",)),
    )(*ins)


def _norm_proj_kernel(x_ref, g_ref, *refs, n_out):
    w_refs, o_refs = refs[:n_out], refs[n_out:]
    hn = _rms(x_ref[...], g_ref[...]).astype(_MXU_DTYPE)
    for w_ref, o_ref in zip(w_refs, o_refs):
        o_ref[...] = jnp.dot(hn, w_ref[...], preferred_element_type=_F32)


def _norm_proj(x, g, ws):
    m = x.shape[0]
    tm = _row_tile(m, 256)
    row = lambda n: pl.BlockSpec((tm, n), lambda i: (i, 0))
    return pl.pallas_call(
        functools.partial(_norm_proj_kernel, n_out=len(ws)),
        grid=(m // tm,),
        in_specs=[row(D_MODEL), _full_spec((1, D_MODEL))] + [_full_spec(w.shape) for w in ws],
        out_specs=[row(w.shape[1]) for w in ws],
        out_shape=[jax.ShapeDtypeStruct((m, w.shape[1]), _F32) for w in ws],
        compiler_params=_params(("parallel",)),
    )(x, g.reshape(1, D_MODEL), *ws)


def _seq_spec(L, n):
    return pl.BlockSpec((None, L, n), lambda b, c: (b, c, 0))


def _state_spec(shape):
    nd = len(shape)
    return pl.BlockSpec((None,) + tuple(shape[1:]), lambda b, c: (b,) + (0,) * (nd - 1))


def _conv_silu(x_ref, w_ref, bias, buf0_ref, win_ref, tail_ref, L):
    lo = CONV_PAD - (CONV_W - 1)

    @pl.when(pl.program_id(1) == 0)
    def _():
        win_ref[lo:CONV_PAD, :] = buf0_ref[...]

    win_ref[CONV_PAD:CONV_PAD + L, :] = x_ref[...]
    y = w_ref[0:1, :] * win_ref[lo:lo + L, :]
    for i in range(1, CONV_W):
        y = y + w_ref[i:i + 1, :] * win_ref[lo + i:lo + i + L, :]
    if bias is not None:
        y = y + bias
    tail = win_ref[L + lo:L + CONV_PAD, :]
    win_ref[lo:CONV_PAD, :] = tail
    tail_ref[...] = tail
    return _silu(y)


def _mlstm_kernel(q_ref, k_ref, v_ref, o_ref, g_ref, bi_ref, bf_ref, nw_ref, c0_ref, n0_ref, m0_ref,
                  h_ref, c_ref, n_ref, m_ref, *, L):
    @pl.when(pl.program_id(1) == 0)
    def _():
        c_ref[...] = c0_ref[...]
        n_ref[...] = n0_ref[...]
        m_ref[...] = m0_ref[...]

    gates = g_ref[...]
    ig = gates + bi_ref[...]
    b = _cumsum_rows(_log_sigmoid(gates + bf_ref[...]))
    ig_t, b_t = ig.T, b.T
    causal = _iota2((L, L), 0) >= _iota2((L, L), 1)
    lane = _iota2((1, NH_A), 1)
    m_old = m_ref[...]
    m_next = m_old
    for h in range(NH_A):
        q = q_ref[:, h * DK_A:(h + 1) * DK_A]
        k = k_ref[:, h * DK_A:(h + 1) * DK_A] * (DK_A ** -0.5)
        v = v_ref[:, h * DV_A:(h + 1) * DV_A]
        b_c = b[:, NH_A + h:NH_A + h + 1]
        ig_c = ig[:, h:h + 1]
        row = ig_t[h:h + 1, :] - b_t[NH_A + h:NH_A + h + 1, :]
        m_prev = m_old[:, h:h + 1]
        dmat = jnp.where(causal, b_c + row, -jnp.inf)
        inter = b_c + m_prev
        m_s = jnp.maximum(inter, jnp.max(dmat, axis=1, keepdims=True))
        scores = _mm_nt(q, k) * jnp.exp(dmat - m_s)
        w_inter = jnp.exp(inter - m_s)
        num = _mm(scores, v) + w_inter * _mm(q, c_ref[h])
        den = (jnp.sum(scores, axis=1, keepdims=True)
               + w_inter * jnp.sum(q * n_ref[h:h + 1, :], axis=1, keepdims=True))
        hh = num / jnp.maximum(jnp.abs(den), jnp.exp(-m_s))
        hh = _rms(hh, nw_ref[h:h + 1, :])
        gated = _sigmoid(o_ref[:, h * DV_A:(h + 1) * DV_A]) * hh
        h_ref[:, h * DV_A:(h + 1) * DV_A] = gated.astype(h_ref.dtype)
        b_last = b_c[L - 1:L, :]
        dl = b_last - b_c + ig_c
        m_new = jnp.maximum(b_last + m_prev, jnp.max(dl, axis=0, keepdims=True))
        kw = k * jnp.exp(dl - m_new)
        decay = jnp.exp(b_last + m_prev - m_new)
        c_ref[h] = decay * c_ref[h] + _mm_tn(kw, v)
        n_ref[h:h + 1, :] = decay * n_ref[h:h + 1, :] + jnp.sum(kw, axis=0, keepdims=True)
        m_next = jnp.where(lane == h, m_new, m_next)
    m_ref[...] = m_next


def _mlstm(q, k, v, o, gates, bi_row, bf_row, norm_w, c0, n0, m0, L):
    bsz, T, _ = q.shape
    m0 = m0.reshape(bsz, 1, NH_A)
    row128 = _full_spec((1, LANES))
    h, c1, n1, m1 = pl.pallas_call(
        functools.partial(_mlstm_kernel, L=L),
        grid=(bsz, T // L),
        in_specs=[_seq_spec(L, NH_A * DK_A), _seq_spec(L, NH_A * DK_A), _seq_spec(L, D_A), _seq_spec(L, D_A),
                  _seq_spec(L, LANES), row128, row128, _full_spec((NH_A, DV_A)),
                  _state_spec(c0.shape), _state_spec(n0.shape), _state_spec(m0.shape)],
        out_specs=[_seq_spec(L, D_A), _state_spec(c0.shape), _state_spec(n0.shape), _state_spec(m0.shape)],
        out_shape=[jax.ShapeDtypeStruct((bsz, T, D_A), _MXU_DTYPE),
                   jax.ShapeDtypeStruct(c0.shape, _F32), jax.ShapeDtypeStruct(n0.shape, _F32),
                   jax.ShapeDtypeStruct(m0.shape, _F32)],
        compiler_params=_params(("parallel", "arbitrary")),
    )(q, k, v, o, gates, bi_row, bf_row, norm_w, c0, n0, m0)
    return h, c1, n1, m1.reshape(bsz, NH_A)


def _ssd_kernel(x_ref, z_ref, g_ref, cw_ref, cb_ref, dtb_ref, alog_ref, dsk_ref, nw_ref, s0_ref, buf0_ref,
                y_ref, s_ref, tail_ref, win_ref, *, L):
    @pl.when(pl.program_id(1) == 0)
    def _():
        s_ref[...] = s0_ref[...]

    act = _conv_silu(x_ref, cw_ref, cb_ref[...], buf0_ref, win_ref, tail_ref, L)
    dt = _softplus(g_ref[...] + dtb_ref[...])
    b = _cumsum_rows(-jnp.exp(alog_ref[...]) * dt)
    eb = jnp.exp(b)
    b_t = b.T
    causal = _iota2((L, L), 0) >= _iota2((L, L), 1)
    lane0 = 2 * NH_A
    for g in range(NG_B):
        bm = act[:, D_B + g * D_STATE:D_B + (g + 1) * D_STATE]
        cm = act[:, D_B + (NG_B + g) * D_STATE:D_B + (NG_B + g + 1) * D_STATE]
        cb = _mm_nt(cm, bm)
        s_g = s_ref[g * HPG_B:(g + 1) * HPG_B].reshape(HPG_B * HEADDIM_B, D_STATE)
        y_inter = _mm_nt(cm, s_g)
        ys, xws = [], []
        for j in range(HPG_B):
            hd = g * HPG_B + j
            ln = lane0 + hd
            x = act[:, hd * HEADDIM_B:(hd + 1) * HEADDIM_B]
            b_c = b[:, ln:ln + 1]
            dt_c = dt[:, ln:ln + 1]
            seg = jnp.exp(jnp.where(causal, b_c - b_t[ln:ln + 1, :], -jnp.inf))
            y = _mm(cb * seg, dt_c * x) + eb[:, ln:ln + 1] * y_inter[:, j * HEADDIM_B:(j + 1) * HEADDIM_B]
            ys.append(y + dsk_ref[:, hd * HEADDIM_B:(hd + 1) * HEADDIM_B] * x)
            b_last = b_c[L - 1:L, :]
            xws.append((jnp.exp(b_last - b_c) * dt_c) * x)
        upd = _mm_tn(jnp.concatenate(xws, axis=1), bm)
        for j in range(HPG_B):
            hd = g * HPG_B + j
            ln = lane0 + hd
            s_ref[hd] = eb[L - 1:L, ln:ln + 1] * s_ref[hd] + upd[j * HEADDIM_B:(j + 1) * HEADDIM_B, :]
        half = D_B // NG_B
        yg = jnp.concatenate(ys, axis=1) * _silu(z_ref[:, g * half:(g + 1) * half])
        y_ref[:, g * half:(g + 1) * half] = _rms(yg, nw_ref[:, g * half:(g + 1) * half]).astype(y_ref.dtype)


def _ssd(xbc, z, gates, conv_w, conv_b, dtb_row, alog_row, dskip_row, norm_w, s0, buf0, L):
    bsz, T, _ = xbc.shape
    row128 = _full_spec((1, LANES))
    return pl.pallas_call(
        functools.partial(_ssd_kernel, L=L),
        grid=(bsz, T // L),
        in_specs=[_seq_spec(L, CONV_DIM_B), _seq_spec(L, D_B), _seq_spec(L, LANES),
                  _full_spec((CONV_W, CONV_DIM_B)), _full_spec((1, CONV_DIM_B)), row128, row128,
                  _full_spec((1, D_B)), _full_spec((1, D_B)),
                  _state_spec(s0.shape), _state_spec(buf0.shape)],
        out_specs=[_seq_spec(L, D_B), _state_spec(s0.shape), _state_spec(buf0.shape)],
        out_shape=[jax.ShapeDtypeStruct((bsz, T, D_B), _MXU_DTYPE), jax.ShapeDtypeStruct(s0.shape, _F32),
                   jax.ShapeDtypeStruct(buf0.shape, _F32)],
        scratch_shapes=[pltpu.VMEM((CONV_PAD + L, CONV_DIM_B), _F32)],
        compiler_params=_params(("parallel", "arbitrary")),
    )(xbc, z, gates, conv_w, conv_b, dtb_row, alog_row, dskip_row, norm_w, s0, buf0)


def _split_lo(a):
    hi = a.astype(_MXU_DTYPE)
    lo = (a - hi.astype(_F32)).astype(_MXU_DTYPE)
    return hi, lo


def _bmm(a, b):
    return jnp.einsum("hij,hjk->hik", a.astype(_MXU_DTYPE), b.astype(_MXU_DTYPE), preferred_element_type=_F32)


def _bmm_nt(a, b):
    return jnp.einsum("hik,hjk->hij", a.astype(_MXU_DTYPE), b.astype(_MXU_DTYPE), preferred_element_type=_F32)


def _bmm_split(a, b):
    (ah, al), (bh, bl) = a, b
    return _bmm(ah, bh) + _bmm(ah, bl) + _bmm(al, bh)


def _unit_lower_inverse(a, L):
    r, c = _iota2((1, L, L), 1), _iota2((1, L, L), 2)
    eye = (r == c).astype(_F32)
    blk = min(INV_BLOCK, L)
    shift = int(math.log2(blk))
    d = jnp.where((r >> shift) == (c >> shift), a, 0.0)
    x = eye - d
    p = _split_lo(d)
    for _ in range(shift - 1):
        p = _split_lo(_bmm_split(p, p))
        x = x + _bmm_split(_split_lo(x), p)
    while (1 << shift) < L:
        off = jnp.where(((r >> (shift + 1)) == (c >> (shift + 1))) & ((r >> shift) != (c >> shift)), a, 0.0)
        xs = _split_lo(x)
        x = x - _bmm_split(_split_lo(_bmm_split(xs, _split_lo(off))), xs)
        shift += 1
    return x


def _gdn_kernel(x_ref, z_ref, g_ref, cw_ref, dtb_ref, alog_ref, nw_ref, s0_ref, buf0_ref,
                o_ref, s_ref, tail_ref, win_ref, *, L, bsz):
    lo = CONV_PAD - (CONV_W - 1)

    @pl.when(pl.program_id(0) == 0)
    def _():
        s_ref[...] = s0_ref[...]
        win_ref[:, lo:CONV_PAD, :] = buf0_ref[...]

    win_ref[:, CONV_PAD:CONV_PAD + L, :] = x_ref[...]
    y = cw_ref[0:1, :] * win_ref[:, lo:lo + L, :]
    for i in range(1, CONV_W):
        y = y + cw_ref[i:i + 1, :] * win_ref[:, lo + i:lo + i + L, :]
    tail = win_ref[:, L + lo:L + CONV_PAD, :]
    win_ref[:, lo:CONV_PAD, :] = tail
    tail_ref[...] = tail
    act = _silu(y)

    pairs = [(b, h) for b in range(bsz) for h in range(NH_C)]
    heads = lambda a, off, w: jnp.stack([a[b, :, off + h * w:off + (h + 1) * w] for b, h in pairs])
    q, k, v = heads(act, 0, DK_C), heads(act, D_CK, DK_C), heads(act, 2 * D_CK, DV_C)
    q = q * (lax.rsqrt(jnp.sum(q * q, axis=-1, keepdims=True) + EPS) * (DK_C ** -0.5))
    k = k * lax.rsqrt(jnp.sum(k * k, axis=-1, keepdims=True) + EPS)
    gates = jnp.concatenate([g_ref[b] for b in range(bsz)], axis=1)
    tile = lambda row: jnp.concatenate([row] * bsz, axis=1)
    beta = _sigmoid(gates)
    gam = _cumsum_rows(-jnp.exp(tile(alog_ref[...])) * _softplus(gates + tile(dtb_ref[...])))
    gam_t = gam.T
    col = lambda a, off: jnp.stack([a[:, b * LANES + off + h:b * LANES + off + h + 1] for b, h in pairs])
    g_c = col(gam, NH_C)
    g_r = jnp.stack([gam_t[b * LANES + NH_C + h:b * LANES + NH_C + h + 1, :] for b, h in pairs])
    beta_c = col(beta, 0)
    eg_c = jnp.exp(g_c)
    r, c = _iota2((1, L, L), 1), _iota2((1, L, L), 2)
    dec = jnp.exp(jnp.where(r >= c, g_c - g_r, -jnp.inf))
    kb = k.astype(_MXU_DTYPE)
    a_mat = jnp.where(r > c, beta_c * dec * _bmm_nt(kb, kb), 0.0)
    th, tl = _split_lo(_unit_lower_inverse(a_mat, L))
    rhs = jnp.concatenate([beta_c * v, (beta_c * eg_c) * k], axis=-1).astype(_MXU_DTYPE)
    sol = _bmm(th, rhs) + _bmm(tl, rhs)
    s = s_ref[...].reshape(bsz * NH_C, DK_C, DV_C)
    sb = s.astype(_MXU_DTYPE)
    u = sol[:, :, :DV_C] - _bmm(sol[:, :, DV_C:], sb)
    o = eg_c * _bmm(q, sb) + _bmm(_bmm_nt(q, kb) * dec, u)
    g_last = g_c[:, L - 1:L, :]
    kw = jnp.exp(g_last - g_c) * k
    o = _rms(o, nw_ref[...]) * _silu(heads(z_ref[...], 0, DV_C))
    for i, (b, h) in enumerate(pairs):
        s_ref[b, h] = jnp.exp(g_last[i]) * s[i] + _mm_tn(kw[i], u[i])
        o_ref[b, :, h * DV_C:(h + 1) * DV_C] = o[i].astype(o_ref.dtype)


def _gdn(qkv, z, gates, conv_w, dtb_row, alog_row, norm_w, s0, buf0, L):
    bsz, T, _ = qkv.shape
    row128 = _full_spec((1, LANES))
    seq = lambda n: pl.BlockSpec((bsz, L, n), lambda c: (0, c, 0))
    return pl.pallas_call(
        functools.partial(_gdn_kernel, L=L, bsz=bsz),
        grid=(T // L,),
        in_specs=[seq(CONV_DIM_C), seq(D_CV), seq(LANES),
                  _full_spec((CONV_W, CONV_DIM_C)), row128, row128, _full_spec((1, DV_C)),
                  _full_spec(s0.shape), _full_spec(buf0.shape)],
        out_specs=[seq(D_CV), _full_spec(s0.shape), _full_spec(buf0.shape)],
        out_shape=[jax.ShapeDtypeStruct((bsz, T, D_CV), _MXU_DTYPE), jax.ShapeDtypeStruct(s0.shape, _F32),
                   jax.ShapeDtypeStruct(buf0.shape, _F32)],
        scratch_shapes=[pltpu.VMEM((bsz, CONV_PAD + L, CONV_DIM_C), _F32)],
        compiler_params=_params(("arbitrary",)),
    )(qkv, z, gates, conv_w, dtb_row, alog_row, norm_w, s0, buf0)


def _lane_row(vals, offset):
    return jnp.zeros((1, LANES), _F32).at[0, offset:offset + vals.shape[0]].set(vals.astype(_F32))


def _split(w, sizes):
    out, off = [], 0
    for s in sizes:
        out.append(w[:, off:off + s])
        off += s
    return out


def _gate_cols(cols):
    w = jnp.concatenate(cols, axis=1)
    return jnp.pad(w, ((0, 0), (0, LANES - w.shape[1])))


def _prepare(norm_g, norm_f, ffn_w_gate, ffn_w_up, ffn_w_down, ab_w_in, ab_w_out, mlstm_b_i, mlstm_b_f,
             mlstm_norm, ssd_conv_w, ssd_conv_b, ssd_dt_bias, ssd_a_log, ssd_d, ssd_norm, gdn_w_in, gdn_w_out,
             gdn_conv_w, gdn_dt_bias, gdn_a_log, gdn_norm):
    lo = lambda w: w.astype(_MXU_DTYPE)
    wq, wk, wv, wo, wig, wfg, wz, wxbc, wdt = _split(ab_w_in, SIZES_AB)
    wqkv, wzc, wb, wa = _split(gdn_w_in, SIZES_C)
    return dict(
        norm_g=norm_g, norm_f=norm_f,
        wg=lo(ffn_w_gate), wu=lo(ffn_w_up), wd=lo(ffn_w_down),
        ab_in=[lo(w) for w in (wq, wk, wv, wo, wz, wxbc, _gate_cols([wig, wfg, wdt]))],
        ab_out=[lo(ab_w_out[:D_A]), lo(ab_w_out[D_A:])],
        bi_row=_lane_row(mlstm_b_i, 0), bf_row=_lane_row(mlstm_b_f, NH_A),
        mlstm_norm=mlstm_norm.astype(_F32),
        ssd_conv_w=ssd_conv_w, ssd_conv_b=ssd_conv_b.reshape(1, CONV_DIM_B),
        ssd_dtb=_lane_row(ssd_dt_bias, 2 * NH_A), ssd_alog=_lane_row(ssd_a_log, 2 * NH_A),
        ssd_dskip=jnp.repeat(ssd_d.astype(_F32), HEADDIM_B).reshape(1, D_B),
        ssd_norm=ssd_norm.reshape(1, D_B),
        c_in=[lo(w) for w in (wqkv, wzc, _gate_cols([wb, wa]))],
        c_out=lo(gdn_w_out),
        gdn_conv_w=gdn_conv_w,
        gdn_dtb=_lane_row(gdn_dt_bias, NH_C), gdn_alog=_lane_row(gdn_a_log, NH_C),
        gdn_norm=gdn_norm.reshape(1, DV_C),
    )


def _trunk(x, ab_state, c_state, p):
    bsz, T, _ = x.shape
    la, lb, lc = min(T, CHUNK_MLSTM), min(T, CHUNK_SSD), min(T, CHUNK_GDN)
    m = bsz * T
    flat = lambda a: a.reshape(m, a.shape[-1])
    seq = lambda a: a.reshape(bsz, T, a.shape[-1])
    c0, n0, m0, s0, bufb0 = ab_state
    sc0, bufc0 = c_state
    x = flat(x)
    ffn = lambda x, l, i, **kw: _ffn(x, p["norm_g"][l, 2 * i], p["wg"], p["wu"], p["wd"], l, i, **kw)
    x = ffn(x, 0, 0)
    q, k, v, o, z, xbc, gates = _norm_proj(x, p["norm_g"][0, 1], p["ab_in"])
    h_a, c1, n1, m1 = _mlstm(seq(q), seq(k), seq(v), seq(o), seq(gates), p["bi_row"], p["bf_row"],
                             p["mlstm_norm"], c0, n0, m0, la)
    y_b, s1, bufb1 = _ssd(seq(xbc), seq(z), seq(gates), p["ssd_conv_w"], p["ssd_conv_b"], p["ssd_dtb"],
                          p["ssd_alog"], p["ssd_dskip"], p["ssd_norm"], s0, bufb0, lb)
    x = ffn(x, 0, 1, pre=([flat(h_a), flat(y_b)], p["ab_out"]))
    x = ffn(x, 1, 0)
    qkv, zc, gates_c = _norm_proj(x, p["norm_g"][1, 1], p["c_in"])
    o_c, sc1, bufc1 = _gdn(seq(qkv), seq(zc), seq(gates_c), p["gdn_conv_w"], p["gdn_dtb"], p["gdn_alog"],
                           p["gdn_norm"], sc0, bufc0, lc)
    y = ffn(x, 1, 1, final_g=p["norm_f"], pre=([flat(o_c)], [p["c_out"]]))
    return seq(y), (c1, n1, m1, s1, bufb1), (sc1, bufc1)


def kernel(x_prompt, x_sample, state_mlstm_C, state_mlstm_n, state_mlstm_m, state_ssd, cache_ssd_conv, state_gdn, cache_gdn_conv, norm_g, norm_f, ffn_w_gate, ffn_w_up, ffn_w_down, ab_w_in, ab_w_out, mlstm_b_i, mlstm_b_f, mlstm_norm, ssd_conv_w, ssd_conv_b, ssd_dt_bias, ssd_a_log, ssd_d, ssd_norm, gdn_w_in, gdn_w_out, gdn_conv_w, gdn_dt_bias, gdn_a_log, gdn_norm):
    p = _prepare(norm_g, norm_f, ffn_w_gate, ffn_w_up, ffn_w_down, ab_w_in, ab_w_out, mlstm_b_i, mlstm_b_f,
                 mlstm_norm, ssd_conv_w, ssd_conv_b, ssd_dt_bias, ssd_a_log, ssd_d, ssd_norm, gdn_w_in,
                 gdn_w_out, gdn_conv_w, gdn_dt_bias, gdn_a_log, gdn_norm)
    bp = x_prompt.shape[0]
    zeros = lambda *s: jnp.zeros(s, _F32)
    ab0 = (zeros(bp, NH_A, DK_A, DV_A), zeros(bp, NH_A, DK_A), zeros(bp, NH_A),
           zeros(bp, NH_B, HEADDIM_B, D_STATE), zeros(bp, CONV_W - 1, CONV_DIM_B))
    c0 = (zeros(bp, NH_C, DK_C, DV_C), zeros(bp, CONV_W - 1, CONV_DIM_C))
    y_p, ab_p, c_p = _trunk(x_prompt, ab0, c0, p)
    y_s, ab_s, c_s = _trunk(x_sample, (state_mlstm_C, state_mlstm_n, state_mlstm_m, state_ssd, cache_ssd_conv),
                            (state_gdn, cache_gdn_conv), p)
    return (y_p, y_s, *ab_p, *c_p, *ab_s, *c_s)
```

```python
import functools
import math

import jax
import jax.numpy as jnp
from jax import lax
from jax.experimental import pallas as pl
from jax.experimental.pallas import tpu as pltpu

D_MODEL = 1024
EPS = 1e-6
CONV_W = 4
NH_A, DK_A, DV_A = 4, 128, 256
D_A = NH_A * DV_A
NH_B, HEADDIM_B, D_STATE, NG_B = 16, 64, 128, 2
D_B = NH_B * HEADDIM_B
HPG_B = NH_B // NG_B
CONV_DIM_B = D_B + 2 * NG_B * D_STATE
NH_C, DK_C, DV_C = 8, 128, 128
D_CK = NH_C * DK_C
D_CV = NH_C * DV_C
CONV_DIM_C = 2 * D_CK + D_CV
D_FF = 2816
SIZES_AB = (NH_A * DK_A, NH_A * DK_A, D_A, D_A, NH_A, NH_A, D_B, CONV_DIM_B, NH_B)
SIZES_C = (CONV_DIM_C, D_CV, NH_C, NH_C)

LANES = 128
CONV_PAD = 8
VMEM_LIMIT = 56 * 1024 * 1024
FFN_COLS = 256
INV_BLOCK = 16
CHUNK_MLSTM, CHUNK_SSD, CHUNK_GDN = 256, 256, 64

_MXU_DTYPE = jnp.bfloat16
_F32 = jnp.float32
_HI = lax.Precision.HIGHEST


def _mm(a, b):
    return jnp.dot(a.astype(_MXU_DTYPE), b.astype(_MXU_DTYPE), preferred_element_type=_F32)


def _mm_nt(a, b):
    return lax.dot_general(a.astype(_MXU_DTYPE), b.astype(_MXU_DTYPE),
                           (((1,), (1,)), ((), ())), preferred_element_type=_F32)


def _mm_tn(a, b):
    return lax.dot_general(a.astype(_MXU_DTYPE), b.astype(_MXU_DTYPE),
                           (((0,), (0,)), ((), ())), preferred_element_type=_F32)


def _mm_hi(a, b):
    return jnp.dot(a, b, precision=_HI, preferred_element_type=_F32)


def _rms(x, g):
    return x * lax.rsqrt(jnp.mean(x * x, axis=-1, keepdims=True) + EPS) * g


def _sigmoid(x):
    return 1.0 / (1.0 + jnp.exp(-x))


def _silu(x):
    return x * _sigmoid(x)


def _softplus(x):
    return jnp.maximum(x, 0.0) + jnp.log1p(jnp.exp(-jnp.abs(x)))


def _log_sigmoid(x):
    return -_softplus(-x)


def _iota2(shape, dim):
    return lax.broadcasted_iota(jnp.int32, shape, dim)


def _cumsum_rows(x):
    n = x.shape[0]
    tri = (_iota2((n, n), 0) >= _iota2((n, n), 1)).astype(_F32)
    return _mm_hi(tri, x)


def _full_spec(shape):
    nd = len(shape)
    return pl.BlockSpec(shape, lambda *_: (0,) * nd)


def _row_tile(m, cap):
    t = min(m, cap)
    assert m % t == 0
    return t


def _params(sem):
    return pltpu.CompilerParams(dimension_semantics=sem, vmem_limit_bytes=VMEM_LIMIT)


def _ffn_kernel(x_ref, g_ref, wg_ref, wu_ref, wd_ref, *rest, final_norm, n_pre):
    pre, rest = rest[:2 * n_pre], rest[2 * n_pre:]
    if final_norm:
        gf_ref, o_ref, h_ref = rest
    else:
        o_ref, h_ref = rest
    x = x_ref[...]
    for a_ref, w_ref in zip(pre[:n_pre], pre[n_pre:]):
        x = x + jnp.dot(a_ref[...].astype(_MXU_DTYPE), w_ref[...], preferred_element_type=_F32)
    hn = _rms(x, g_ref[...]).astype(_MXU_DTYPE)
    for c in range(D_FF // FFN_COLS):
        cols = slice(c * FFN_COLS, (c + 1) * FFN_COLS)
        gate = jnp.dot(hn, wg_ref[:, cols], preferred_element_type=_F32)
        up = jnp.dot(hn, wu_ref[:, cols], preferred_element_type=_F32)
        h_ref[:, cols] = (_silu(gate) * up).astype(_MXU_DTYPE)
    y = x + 0.5 * jnp.dot(h_ref[...], wd_ref[...], preferred_element_type=_F32)
    if final_norm:
        y = _rms(y, gf_ref[...])
    o_ref[...] = y


def _ffn(x, g, wg, wu, wd, layer, slot, final_g=None, pre=None):
    m = x.shape[0]
    tm = _row_tile(m, 512)
    row = lambda n: pl.BlockSpec((tm, n), lambda i: (i, 0))
    once = pl.Buffered(1)
    pick = lambda w: pl.BlockSpec((None, None) + w.shape[2:], lambda i: (layer, slot, 0, 0), pipeline_mode=once)
    acts, ws = pre if pre is not None else ([], [])
    ins = [x, g.reshape(1, D_MODEL), wg, wu, wd, *acts, *ws]
    specs = ([row(D_MODEL), _full_spec((1, D_MODEL)), pick(wg), pick(wu), pick(wd)]
             + [row(a.shape[1]) for a in acts]
             + [pl.BlockSpec(w.shape, lambda i: (0, 0), pipeline_mode=once) for w in ws])
    if final_g is not None:
        ins.append(final_g.reshape(1, D_MODEL))
        specs.append(_full_spec((1, D_MODEL)))
    row = row(D_MODEL)
    return pl.pallas_call(
        functools.partial(_ffn_kernel, final_norm=final_g is not None, n_pre=len(acts)),
        grid=(m // tm,),
        in_specs=specs,
        out_specs=row,
        out_shape=jax.ShapeDtypeStruct((m, D_MODEL), _F32),
        scratch_shapes=[pltpu.VMEM((tm, D_FF), _MXU_DTYPE)],
        compiler_params=_params(("parallel",)),
    )(*ins)


def _norm_proj_kernel(x_ref, g_ref, *refs, n_out):
    w_refs, o_refs = refs[:n_out], refs[n_out:]
    hn = _rms(x_ref[...], g_ref[...]).astype(_MXU_DTYPE)
    for w_ref, o_ref in zip(w_refs, o_refs):
        o_ref[...] = jnp.dot(hn, w_ref[...], preferred_element_type=_F32)


def _norm_proj(x, g, ws):
    m = x.shape[0]
    tm = _row_tile(m, 512)
    row = lambda n: pl.BlockSpec((tm, n), lambda i: (i, 0))
    resident = lambda w: pl.BlockSpec(w.shape, lambda i: (0, 0), pipeline_mode=pl.Buffered(1))
    return pl.pallas_call(
        functools.partial(_norm_proj_kernel, n_out=len(ws)),
        grid=(m // tm,),
        in_specs=[row(D_MODEL), _full_spec((1, D_MODEL))] + [resident(w) for w in ws],
        out_specs=[row(w.shape[1]) for w in ws],
        out_shape=[jax.ShapeDtypeStruct((m, w.shape[1]), _F32) for w in ws],
        compiler_params=_params(("parallel",)),
    )(x, g.reshape(1, D_MODEL), *ws)


def _seq_spec(L, n):
    return pl.BlockSpec((None, L, n), lambda b, c: (b, c, 0))


def _state_spec(shape):
    nd = len(shape)
    return pl.BlockSpec((None,) + tuple(shape[1:]), lambda b, c: (b,) + (0,) * (nd - 1))


def _conv_silu(x_ref, w_ref, bias, buf0_ref, win_ref, tail_ref, L):
    lo = CONV_PAD - (CONV_W - 1)

    @pl.when(pl.program_id(1) == 0)
    def _():
        win_ref[lo:CONV_PAD, :] = buf0_ref[...]

    win_ref[CONV_PAD:CONV_PAD + L, :] = x_ref[...]
    y = w_ref[0:1, :] * win_ref[lo:lo + L, :]
    for i in range(1, CONV_W):
        y = y + w_ref[i:i + 1, :] * win_ref[lo + i:lo + i + L, :]
    if bias is not None:
        y = y + bias
    tail = win_ref[L + lo:L + CONV_PAD, :]
    win_ref[lo:CONV_PAD, :] = tail
    tail_ref[...] = tail
    return _silu(y)


def _mlstm_kernel(q_ref, k_ref, v_ref, o_ref, g_ref, bi_ref, bf_ref, nw_ref, c0_ref, n0_ref, m0_ref,
                  h_ref, c_ref, n_ref, m_ref, *, L):
    @pl.when(pl.program_id(1) == 0)
    def _():
        c_ref[...] = c0_ref[...]
        n_ref[...] = n0_ref[...]
        m_ref[...] = m0_ref[...]

    gates = g_ref[...]
    ig = gates + bi_ref[...]
    b = _cumsum_rows(_log_sigmoid(gates + bf_ref[...]))
    ig_t, b_t = ig.T, b.T
    causal = _iota2((L, L), 0) >= _iota2((L, L), 1)
    lane = _iota2((1, NH_A), 1)
    m_old = m_ref[...]
    m_next = m_old
    for h in range(NH_A):
        q = q_ref[:, h * DK_A:(h + 1) * DK_A]
        k = k_ref[:, h * DK_A:(h + 1) * DK_A] * (DK_A ** -0.5)
        v = v_ref[:, h * DV_A:(h + 1) * DV_A]
        b_c = b[:, NH_A + h:NH_A + h + 1]
        ig_c = ig[:, h:h + 1]
        row = ig_t[h:h + 1, :] - b_t[NH_A + h:NH_A + h + 1, :]
        m_prev = m_old[:, h:h + 1]
        dmat = jnp.where(causal, b_c + row, -jnp.inf)
        inter = b_c + m_prev
        m_s = jnp.maximum(inter, jnp.max(dmat, axis=1, keepdims=True))
        scores = _mm_nt(q, k) * jnp.exp(dmat - m_s)
        w_inter = jnp.exp(inter - m_s)
        num = _mm(scores, v) + w_inter * _mm(q, c_ref[h])
        den = (jnp.sum(scores, axis=1, keepdims=True)
               + w_inter * jnp.sum(q * n_ref[h:h + 1, :], axis=1, keepdims=True))
        hh = num / jnp.maximum(jnp.abs(den), jnp.exp(-m_s))
        hh = _rms(hh, nw_ref[h:h + 1, :])
        gated = _sigmoid(o_ref[:, h * DV_A:(h + 1) * DV_A]) * hh
        h_ref[:, h * DV_A:(h + 1) * DV_A] = gated.astype(h_ref.dtype)
        b_last = b_c[L - 1:L, :]
        dl = b_last - b_c + ig_c
        m_new = jnp.maximum(b_last + m_prev, jnp.max(dl, axis=0, keepdims=True))
        kw = k * jnp.exp(dl - m_new)
        decay = jnp.exp(b_last + m_prev - m_new)
        c_ref[h] = decay * c_ref[h] + _mm_tn(kw, v)
        n_ref[h:h + 1, :] = decay * n_ref[h:h + 1, :] + jnp.sum(kw, axis=0, keepdims=True)
        m_next = jnp.where(lane == h, m_new, m_next)
    m_ref[...] = m_next


def _mlstm(q, k, v, o, gates, bi_row, bf_row, norm_w, c0, n0, m0, L):
    bsz, T, _ = q.shape
    m0 = m0.reshape(bsz, 1, NH_A)
    row128 = _full_spec((1, LANES))
    h, c1, n1, m1 = pl.pallas_call(
        functools.partial(_mlstm_kernel, L=L),
        grid=(bsz, T // L),
        in_specs=[_seq_spec(L, NH_A * DK_A), _seq_spec(L, NH_A * DK_A), _seq_spec(L, D_A), _seq_spec(L, D_A),
                  _seq_spec(L, LANES), row128, row128, _full_spec((NH_A, DV_A)),
                  _state_spec(c0.shape), _state_spec(n0.shape), _state_spec(m0.shape)],
        out_specs=[_seq_spec(L, D_A), _state_spec(c0.shape), _state_spec(n0.shape), _state_spec(m0.shape)],
        out_shape=[jax.ShapeDtypeStruct((bsz, T, D_A), _MXU_DTYPE),
                   jax.ShapeDtypeStruct(c0.shape, _F32), jax.ShapeDtypeStruct(n0.shape, _F32),
                   jax.ShapeDtypeStruct(m0.shape, _F32)],
        compiler_params=_params(("parallel", "arbitrary")),
    )(q, k, v, o, gates, bi_row, bf_row, norm_w, c0, n0, m0)
    return h, c1, n1, m1.reshape(bsz, NH_A)


def _ssd_kernel(x_ref, z_ref, g_ref, cw_ref, cb_ref, dtb_ref, alog_ref, dsk_ref, nw_ref, s0_ref, buf0_ref,
                y_ref, s_ref, tail_ref, win_ref, *, L):
    @pl.when(pl.program_id(1) == 0)
    def _():
        s_ref[...] = s0_ref[...]

    act = _conv_silu(x_ref, cw_ref, cb_ref[...], buf0_ref, win_ref, tail_ref, L)
    dt = _softplus(g_ref[...] + dtb_ref[...])
    b = _cumsum_rows(-jnp.exp(alog_ref[...]) * dt)
    eb = jnp.exp(b)
    b_t = b.T
    causal = _iota2((L, L), 0) >= _iota2((L, L), 1)
    lane0 = 2 * NH_A
    for g in range(NG_B):
        bm = act[:, D_B + g * D_STATE:D_B + (g + 1) * D_STATE]
        cm = act[:, D_B + (NG_B + g) * D_STATE:D_B + (NG_B + g + 1) * D_STATE]
        cb = _mm_nt(cm, bm)
        s_g = s_ref[g * HPG_B:(g + 1) * HPG_B].reshape(HPG_B * HEADDIM_B, D_STATE)
        y_inter = _mm_nt(cm, s_g)
        ys, xws = [], []
        for j in range(HPG_B):
            hd = g * HPG_B + j
            ln = lane0 + hd
            x = act[:, hd * HEADDIM_B:(hd + 1) * HEADDIM_B]
            b_c = b[:, ln:ln + 1]
            dt_c = dt[:, ln:ln + 1]
            seg = jnp.exp(jnp.where(causal, b_c - b_t[ln:ln + 1, :], -jnp.inf))
            y = _mm(cb * seg, dt_c * x) + eb[:, ln:ln + 1] * y_inter[:, j * HEADDIM_B:(j + 1) * HEADDIM_B]
            ys.append(y + dsk_ref[:, hd * HEADDIM_B:(hd + 1) * HEADDIM_B] * x)
            b_last = b_c[L - 1:L, :]
            xws.append((jnp.exp(b_last - b_c) * dt_c) * x)
        upd = _mm_tn(jnp.concatenate(xws, axis=1), bm)
        for j in range(HPG_B):
            hd = g * HPG_B + j
            ln = lane0 + hd
            s_ref[hd] = eb[L - 1:L, ln:ln + 1] * s_ref[hd] + upd[j * HEADDIM_B:(j + 1) * HEADDIM_B, :]
        half = D_B // NG_B
        yg = jnp.concatenate(ys, axis=1) * _silu(z_ref[:, g * half:(g + 1) * half])
        y_ref[:, g * half:(g + 1) * half] = _rms(yg, nw_ref[:, g * half:(g + 1) * half]).astype(y_ref.dtype)


def _ssd(xbc, z, gates, conv_w, conv_b, dtb_row, alog_row, dskip_row, norm_w, s0, buf0, L):
    bsz, T, _ = xbc.shape
    row128 = _full_spec((1, LANES))
    return pl.pallas_call(
        functools.partial(_ssd_kernel, L=L),
        grid=(bsz, T // L),
        in_specs=[_seq_spec(L, CONV_DIM_B), _seq_spec(L, D_B), _seq_spec(L, LANES),
                  _full_spec((CONV_W, CONV_DIM_B)), _full_spec((1, CONV_DIM_B)), row128, row128,
                  _full_spec((1, D_B)), _full_spec((1, D_B)),
                  _state_spec(s0.shape), _state_spec(buf0.shape)],
        out_specs=[_seq_spec(L, D_B), _state_spec(s0.shape), _state_spec(buf0.shape)],
        out_shape=[jax.ShapeDtypeStruct((bsz, T, D_B), _MXU_DTYPE), jax.ShapeDtypeStruct(s0.shape, _F32),
                   jax.ShapeDtypeStruct(buf0.shape, _F32)],
        scratch_shapes=[pltpu.VMEM((CONV_PAD + L, CONV_DIM_B), _F32)],
        compiler_params=_params(("parallel", "arbitrary")),
    )(xbc, z, gates, conv_w, conv_b, dtb_row, alog_row, dskip_row, norm_w, s0, buf0)


def _split_lo(a):
    hi = a.astype(_MXU_DTYPE)
    lo = (a - hi.astype(_F32)).astype(_MXU_DTYPE)
    return hi, lo


def _bmm(a, b):
    return jnp.einsum("hij,hjk->hik", a.astype(_MXU_DTYPE), b.astype(_MXU_DTYPE), preferred_element_type=_F32)


def _bmm_nt(a, b):
    return jnp.einsum("hik,hjk->hij", a.astype(_MXU_DTYPE), b.astype(_MXU_DTYPE), preferred_element_type=_F32)


def _bmm_split(a, b):
    (ah, al), (bh, bl) = a, b
    return _bmm(ah, bh) + _bmm(ah, bl) + _bmm(al, bh)


def _unit_lower_inverse(a, L):
    r, c = _iota2((1, L, L), 1), _iota2((1, L, L), 2)
    eye = (r == c).astype(_F32)
    blk = min(INV_BLOCK, L)
    shift = int(math.log2(blk))
    d = jnp.where((r >> shift) == (c >> shift), a, 0.0)
    x = eye - d
    p = _split_lo(d)
    for _ in range(shift - 1):
        p = _split_lo(_bmm_split(p, p))
        x = x + _bmm_split(_split_lo(x), p)
    while (1 << shift) < L:
        off = jnp.where(((r >> (shift + 1)) == (c >> (shift + 1))) & ((r >> shift) != (c >> shift)), a, 0.0)
        xs = _split_lo(x)
        x = x - _bmm_split(_split_lo(_bmm_split(xs, _split_lo(off))), xs)
        shift += 1
    return x


def _gdn_kernel(x_ref, z_ref, g_ref, cw_ref, dtb_ref, alog_ref, nw_ref, s0_ref, buf0_ref,
                o_ref, s_ref, tail_ref, win_ref, *, L, bsz):
    lo = CONV_PAD - (CONV_W - 1)

    @pl.when(pl.program_id(0) == 0)
    def _():
        s_ref[...] = s0_ref[...]
        win_ref[:, lo:CONV_PAD, :] = buf0_ref[...]

    win_ref[:, CONV_PAD:CONV_PAD + L, :] = x_ref[...]
    y = cw_ref[0:1, :] * win_ref[:, lo:lo + L, :]
    for i in range(1, CONV_W):
        y = y + cw_ref[i:i + 1, :] * win_ref[:, lo + i:lo + i + L, :]
    tail = win_ref[:, L + lo:L + CONV_PAD, :]
    win_ref[:, lo:CONV_PAD, :] = tail
    tail_ref[...] = tail
    act = _silu(y)

    pairs = [(b, h) for b in range(bsz) for h in range(NH_C)]
    heads = lambda a, off, w: jnp.stack([a[b, :, off + h * w:off + (h + 1) * w] for b, h in pairs])
    q, k, v = heads(act, 0, DK_C), heads(act, D_CK, DK_C), heads(act, 2 * D_CK, DV_C)
    q = q * (lax.rsqrt(jnp.sum(q * q, axis=-1, keepdims=True) + EPS) * (DK_C ** -0.5))
    k = k * lax.rsqrt(jnp.sum(k * k, axis=-1, keepdims=True) + EPS)
    gates = jnp.concatenate([g_ref[b] for b in range(bsz)], axis=1)
    tile = lambda row: jnp.concatenate([row] * bsz, axis=1)
    beta = _sigmoid(gates)
    gam = _cumsum_rows(-jnp.exp(tile(alog_ref[...])) * _softplus(gates + tile(dtb_ref[...])))
    gam_t = gam.T
    col = lambda a, off: jnp.stack([a[:, b * LANES + off + h:b * LANES + off + h + 1] for b, h in pairs])
    g_c = col(gam, NH_C)
    g_r = jnp.stack([gam_t[b * LANES + NH_C + h:b * LANES + NH_C + h + 1, :] for b, h in pairs])
    beta_c = col(beta, 0)
    eg_c = jnp.exp(g_c)
    r, c = _iota2((1, L, L), 1), _iota2((1, L, L), 2)
    dec = jnp.exp(jnp.where(r >= c, g_c - g_r, -jnp.inf))
    kb = k.astype(_MXU_DTYPE)
    a_mat = jnp.where(r > c, beta_c * dec * _bmm_nt(kb, kb), 0.0)
    th, tl = _split_lo(_unit_lower_inverse(a_mat, L))
    rhs = jnp.concatenate([beta_c * v, (beta_c * eg_c) * k], axis=-1).astype(_MXU_DTYPE)
    sol = _bmm(th, rhs) + _bmm(tl, rhs)
    s = s_ref[...].reshape(bsz * NH_C, DK_C, DV_C)
    sb = s.astype(_MXU_DTYPE)
    u = sol[:, :, :DV_C] - _bmm(sol[:, :, DV_C:], sb)
    o = eg_c * _bmm(q, sb) + _bmm(_bmm_nt(q, kb) * dec, u)
    g_last = g_c[:, L - 1:L, :]
    kw = jnp.exp(g_last - g_c) * k
    o = _rms(o, nw_ref[...]) * _silu(heads(z_ref[...], 0, DV_C))
    for i, (b, h) in enumerate(pairs):
        s_ref[b, h] = jnp.exp(g_last[i]) * s[i] + _mm_tn(kw[i], u[i])
        o_ref[b, :, h * DV_C:(h + 1) * DV_C] = o[i].astype(o_ref.dtype)


def _gdn(qkv, z, gates, conv_w, dtb_row, alog_row, norm_w, s0, buf0, L):
    bsz, T, _ = qkv.shape
    row128 = _full_spec((1, LANES))
    seq = lambda n: pl.BlockSpec((bsz, L, n), lambda c: (0, c, 0))
    return pl.pallas_call(
        functools.partial(_gdn_kernel, L=L, bsz=bsz),
        grid=(T // L,),
        in_specs=[seq(CONV_DIM_C), seq(D_CV), seq(LANES),
                  _full_spec((CONV_W, CONV_DIM_C)), row128, row128, _full_spec((1, DV_C)),
                  _full_spec(s0.shape), _full_spec(buf0.shape)],
        out_specs=[seq(D_CV), _full_spec(s0.shape), _full_spec(buf0.shape)],
        out_shape=[jax.ShapeDtypeStruct((bsz, T, D_CV), _MXU_DTYPE), jax.ShapeDtypeStruct(s0.shape, _F32),
                   jax.ShapeDtypeStruct(buf0.shape, _F32)],
        scratch_shapes=[pltpu.VMEM((bsz, CONV_PAD + L, CONV_DIM_C), _F32)],
        compiler_params=_params(("arbitrary",)),
    )(qkv, z, gates, conv_w, dtb_row, alog_row, norm_w, s0, buf0)


def _lane_row(vals, offset):
    return jnp.zeros((1, LANES), _F32).at[0, offset:offset + vals.shape[0]].set(vals.astype(_F32))


def _split(w, sizes):
    out, off = [], 0
    for s in sizes:
        out.append(w[:, off:off + s])
        off += s
    return out


def _gate_cols(cols):
    w = jnp.concatenate(cols, axis=1)
    return jnp.pad(w, ((0, 0), (0, LANES - w.shape[1])))


def _prepare(norm_g, norm_f, ffn_w_gate, ffn_w_up, ffn_w_down, ab_w_in, ab_w_out, mlstm_b_i, mlstm_b_f,
             mlstm_norm, ssd_conv_w, ssd_conv_b, ssd_dt_bias, ssd_a_log, ssd_d, ssd_norm, gdn_w_in, gdn_w_out,
             gdn_conv_w, gdn_dt_bias, gdn_a_log, gdn_norm):
    lo = lambda w: w.astype(_MXU_DTYPE)
    wq, wk, wv, wo, wig, wfg, wz, wxbc, wdt = _split(ab_w_in, SIZES_AB)
    wqkv, wzc, wb, wa = _split(gdn_w_in, SIZES_C)
    return dict(
        norm_g=norm_g, norm_f=norm_f,
        wg=lo(ffn_w_gate), wu=lo(ffn_w_up), wd=lo(ffn_w_down),
        ab_in=[lo(w) for w in (wq, wk, wv, wo, wz, wxbc, _gate_cols([wig, wfg, wdt]))],
        ab_out=[lo(ab_w_out[:D_A]), lo(ab_w_out[D_A:])],
        bi_row=_lane_row(mlstm_b_i, 0), bf_row=_lane_row(mlstm_b_f, NH_A),
        mlstm_norm=mlstm_norm.astype(_F32),
        ssd_conv_w=ssd_conv_w, ssd_conv_b=ssd_conv_b.reshape(1, CONV_DIM_B),
        ssd_dtb=_lane_row(ssd_dt_bias, 2 * NH_A), ssd_alog=_lane_row(ssd_a_log, 2 * NH_A),
        ssd_dskip=jnp.repeat(ssd_d.astype(_F32), HEADDIM_B).reshape(1, D_B),
        ssd_norm=ssd_norm.reshape(1, D_B),
        c_in=[lo(w) for w in (wqkv, wzc, _gate_cols([wb, wa]))],
        c_out=lo(gdn_w_out),
        gdn_conv_w=gdn_conv_w,
        gdn_dtb=_lane_row(gdn_dt_bias, NH_C), gdn_alog=_lane_row(gdn_a_log, NH_C),
        gdn_norm=gdn_norm.reshape(1, DV_C),
    )


def _trunk(x, ab_state, c_state, p):
    bsz, T, _ = x.shape
    la, lb, lc = min(T, CHUNK_MLSTM), min(T, CHUNK_SSD), min(T, CHUNK_GDN)
    m = bsz * T
    flat = lambda a: a.reshape(m, a.shape[-1])
    seq = lambda a: a.reshape(bsz, T, a.shape[-1])
    c0, n0, m0, s0, bufb0 = ab_state
    sc0, bufc0 = c_state
    x = flat(x)
    ffn = lambda x, l, i, **kw: _ffn(x, p["norm_g"][l, 2 * i], p["wg"], p["wu"], p["wd"], l, i, **kw)
    x = ffn(x, 0, 0)
    q, k, v, o, z, xbc, gates = _norm_proj(x, p["norm_g"][0, 1], p["ab_in"])
    h_a, c1, n1, m1 = _mlstm(seq(q), seq(k), seq(v), seq(o), seq(gates), p["bi_row"], p["bf_row"],
                             p["mlstm_norm"], c0, n0, m0, la)
    y_b, s1, bufb1 = _ssd(seq(xbc), seq(z), seq(gates), p["ssd_conv_w"], p["ssd_conv_b"], p["ssd_dtb"],
                          p["ssd_alog"], p["ssd_dskip"], p["ssd_norm"], s0, bufb0, lb)
    x = ffn(x, 0, 1, pre=([flat(h_a), flat(y_b)], p["ab_out"]))
    x = ffn(x, 1, 0)
    qkv, zc, gates_c = _norm_proj(x, p["norm_g"][1, 1], p["c_in"])
    o_c, sc1, bufc1 = _gdn(seq(qkv), seq(zc), seq(gates_c), p["gdn_conv_w"], p["gdn_dtb"], p["gdn_alog"],
                           p["gdn_norm"], sc0, bufc0, lc)
    y = ffn(x, 1, 1, final_g=p["norm_f"], pre=([flat(o_c)], [p["c_out"]]))
    return seq(y), (c1, n1, m1, s1, bufb1), (sc1, bufc1)


def kernel(x_prompt, x_sample, state_mlstm_C, state_mlstm_n, state_mlstm_m, state_ssd, cache_ssd_conv, state_gdn, cache_gdn_conv, norm_g, norm_f, ffn_w_gate, ffn_w_up, ffn_w_down, ab_w_in, ab_w_out, mlstm_b_i, mlstm_b_f, mlstm_norm, ssd_conv_w, ssd_conv_b, ssd_dt_bias, ssd_a_log, ssd_d, ssd_norm, gdn_w_in, gdn_w_out, gdn_conv_w, gdn_dt_bias, gdn_a_log, gdn_norm):
    p = _prepare(norm_g, norm_f, ffn_w_gate, ffn_w_up, ffn_w_down, ab_w_in, ab_w_out, mlstm_b_i, mlstm_b_f,
                 mlstm_norm, ssd_conv_w, ssd_conv_b, ssd_dt_bias, ssd_a_log, ssd_d, ssd_norm, gdn_w_in,
                 gdn_w_out, gdn_conv_w, gdn_dt_bias, gdn_a_log, gdn_norm)
    bp = x_prompt.shape[0]
    zeros = lambda *s: jnp.zeros(s, _F32)
    ab0 = (zeros(bp, NH_A, DK_A, DV_A), zeros(bp, NH_A, DK_A), zeros(bp, NH_A),
           zeros(bp, NH_B, HEADDIM_B, D_STATE), zeros(bp, CONV_W - 1, CONV_DIM_B))
    c0 = (zeros(bp, NH_C, DK_C, DV_C), zeros(bp, CONV_W - 1, CONV_DIM_C))
    y_p, ab_p, c_p = _trunk(x_prompt, ab0, c0, p)
    y_s, ab_s, c_s = _trunk(x_sample, (state_mlstm_C, state_mlstm_n, state_mlstm_m, state_ssd, cache_ssd_conv),
                            (state_gdn, cache_gdn_conv), p)
    return (y_p, y_s, *ab_p, *c_p, *ab_s, *c_s)
```

```python
import functools
import math

import jax
import jax.numpy as jnp
from jax import lax
from jax.experimental import pallas as pl
from jax.experimental.pallas import tpu as pltpu

D_MODEL = 1024
EPS = 1e-6
CONV_W = 4
NH_A, DK_A, DV_A = 4, 128, 256
D_A = NH_A * DV_A
NH_B, HEADDIM_B, D_STATE, NG_B = 16, 64, 128, 2
D_B = NH_B * HEADDIM_B
HPG_B = NH_B // NG_B
CONV_DIM_B = D_B + 2 * NG_B * D_STATE
NH_C, DK_C, DV_C = 8, 128, 128
D_CK = NH_C * DK_C
D_CV = NH_C * DV_C
CONV_DIM_C = 2 * D_CK + D_CV
D_FF = 2816
SIZES_AB = (NH_A * DK_A, NH_A * DK_A, D_A, D_A, NH_A, NH_A, D_B, CONV_DIM_B, NH_B)
SIZES_C = (CONV_DIM_C, D_CV, NH_C, NH_C)

LANES = 128
CONV_PAD = 8
VMEM_LIMIT = 56 * 1024 * 1024
FFN_COLS = 256
INV_BLOCK = 16
CHUNK_MLSTM, CHUNK_SSD, CHUNK_GDN = 256, 256, 64

_MXU_DTYPE = jnp.bfloat16
_F32 = jnp.float32
_HI = lax.Precision.HIGHEST


def _mm(a, b):
    return jnp.dot(a.astype(_MXU_DTYPE), b.astype(_MXU_DTYPE), preferred_element_type=_F32)


def _mm_nt(a, b):
    return lax.dot_general(a.astype(_MXU_DTYPE), b.astype(_MXU_DTYPE),
                           (((1,), (1,)), ((), ())), preferred_element_type=_F32)


def _mm_tn(a, b):
    return lax.dot_general(a.astype(_MXU_DTYPE), b.astype(_MXU_DTYPE),
                           (((0,), (0,)), ((), ())), preferred_element_type=_F32)


def _mm_hi(a, b):
    return jnp.dot(a, b, precision=_HI, preferred_element_type=_F32)


def _rms(x, g):
    return x * lax.rsqrt(jnp.mean(x * x, axis=-1, keepdims=True) + EPS) * g


def _sigmoid(x):
    return 1.0 / (1.0 + jnp.exp(-x))


def _silu(x):
    return x * _sigmoid(x)


def _softplus(x):
    return jnp.maximum(x, 0.0) + jnp.log1p(jnp.exp(-jnp.abs(x)))


def _log_sigmoid(x):
    return -_softplus(-x)


def _iota2(shape, dim):
    return lax.broadcasted_iota(jnp.int32, shape, dim)


def _cumsum_rows(x):
    n = x.shape[0]
    tri = (_iota2((n, n), 0) >= _iota2((n, n), 1)).astype(_F32)
    return _mm_hi(tri, x)


def _full_spec(shape):
    nd = len(shape)
    return pl.BlockSpec(shape, lambda *_: (0,) * nd)


def _row_tile(m, cap):
    t = min(m, cap)
    assert m % t == 0
    return t


def _params(sem):
    return pltpu.CompilerParams(dimension_semantics=sem, vmem_limit_bytes=VMEM_LIMIT)


def _ffn_kernel(x_ref, g_ref, wg_ref, wu_ref, wd_ref, *rest, final_norm, n_pre):
    pre, rest = rest[:2 * n_pre], rest[2 * n_pre:]
    if final_norm:
        gf_ref, o_ref, h_ref = rest
    else:
        o_ref, h_ref = rest
    x = x_ref[...]
    for a_ref, w_ref in zip(pre[:n_pre], pre[n_pre:]):
        x = x + jnp.dot(a_ref[...].astype(_MXU_DTYPE), w_ref[...], preferred_element_type=_F32)
    hn = _rms(x, g_ref[...]).astype(_MXU_DTYPE)
    for c in range(D_FF // FFN_COLS):
        cols = slice(c * FFN_COLS, (c + 1) * FFN_COLS)
        gate = jnp.dot(hn, wg_ref[:, cols], preferred_element_type=_F32)
        up = jnp.dot(hn, wu_ref[:, cols], preferred_element_type=_F32)
        h_ref[:, cols] = (_silu(gate) * up).astype(_MXU_DTYPE)
    y = x + 0.5 * jnp.dot(h_ref[...], wd_ref[...], preferred_element_type=_F32)
    if final_norm:
        y = _rms(y, gf_ref[...])
    o_ref[...] = y


def _ffn(x, g, wg, wu, wd, layer, slot, final_g=None, pre=None):
    m = x.shape[0]
    tm = _row_tile(m, 512)
    row = lambda n: pl.BlockSpec((tm, n), lambda i: (i, 0))
    once = pl.Buffered(1)
    pick = lambda w: pl.BlockSpec((None, None) + w.shape[2:], lambda i: (layer, slot, 0, 0), pipeline_mode=once)
    acts, ws = pre if pre is not None else ([], [])
    ins = [x, g.reshape(1, D_MODEL), wg, wu, wd, *acts, *ws]
    specs = ([row(D_MODEL), _full_spec((1, D_MODEL)), pick(wg), pick(wu), pick(wd)]
             + [row(a.shape[1]) for a in acts]
             + [pl.BlockSpec(w.shape, lambda i: (0, 0), pipeline_mode=once) for w in ws])
    if final_g is not None:
        ins.append(final_g.reshape(1, D_MODEL))
        specs.append(_full_spec((1, D_MODEL)))
    row = row(D_MODEL)
    return pl.pallas_call(
        functools.partial(_ffn_kernel, final_norm=final_g is not None, n_pre=len(acts)),
        grid=(m // tm,),
        in_specs=specs,
        out_specs=row,
        out_shape=jax.ShapeDtypeStruct((m, D_MODEL), _F32),
        scratch_shapes=[pltpu.VMEM((tm, D_FF), _MXU_DTYPE)],
        compiler_params=_params(("parallel",)),
    )(*ins)


def _norm_proj_kernel(x_ref, g_ref, *refs, n_out):
    w_refs, o_refs = refs[:n_out], refs[n_out:]
    hn = _rms(x_ref[...], g_ref[...]).astype(_MXU_DTYPE)
    for w_ref, o_ref in zip(w_refs, o_refs):
        o_ref[...] = jnp.dot(hn, w_ref[...], preferred_element_type=_F32)


def _norm_proj(x, g, ws):
    m = x.shape[0]
    tm = _row_tile(m, 512)
    row = lambda n: pl.BlockSpec((tm, n), lambda i: (i, 0))
    resident = lambda w: pl.BlockSpec(w.shape, lambda i: (0, 0), pipeline_mode=pl.Buffered(1))
    return pl.pallas_call(
        functools.partial(_norm_proj_kernel, n_out=len(ws)),
        grid=(m // tm,),
        in_specs=[row(D_MODEL), _full_spec((1, D_MODEL))] + [resident(w) for w in ws],
        out_specs=[row(w.shape[1]) for w in ws],
        out_shape=[jax.ShapeDtypeStruct((m, w.shape[1]), _F32) for w in ws],
        compiler_params=_params(("parallel",)),
    )(x, g.reshape(1, D_MODEL), *ws)


def _seq_spec(L, n):
    return pl.BlockSpec((None, L, n), lambda b, c: (b, c, 0))


def _state_spec(shape):
    nd = len(shape)
    return pl.BlockSpec((None,) + tuple(shape[1:]), lambda b, c: (b,) + (0,) * (nd - 1))


def _conv_silu(x_ref, w_ref, bias, buf0_ref, win_ref, tail_ref, L):
    lo = CONV_PAD - (CONV_W - 1)

    @pl.when(pl.program_id(1) == 0)
    def _():
        win_ref[lo:CONV_PAD, :] = buf0_ref[...]

    win_ref[CONV_PAD:CONV_PAD + L, :] = x_ref[...]
    y = w_ref[0:1, :] * win_ref[lo:lo + L, :]
    for i in range(1, CONV_W):
        y = y + w_ref[i:i + 1, :] * win_ref[lo + i:lo + i + L, :]
    if bias is not None:
        y = y + bias
    tail = win_ref[L + lo:L + CONV_PAD, :]
    win_ref[lo:CONV_PAD, :] = tail
    tail_ref[...] = tail
    return _silu(y)


def _mlstm_kernel(q_ref, k_ref, v_ref, o_ref, g_ref, bi_ref, bf_ref, nw_ref, c0_ref, n0_ref, m0_ref,
                  h_ref, c_ref, n_ref, m_ref, *, L, bsz):
    @pl.when(pl.program_id(0) == 0)
    def _():
        c_ref[...] = c0_ref[...]
        n_ref[...] = n0_ref[...]
        m_ref[...] = m0_ref[...]

    for s in range(bsz):
        _mlstm_stream(q_ref.at[s], k_ref.at[s], v_ref.at[s], o_ref.at[s], g_ref.at[s], bi_ref, bf_ref, nw_ref,
                      h_ref.at[s], c_ref.at[s], n_ref.at[s], m_ref.at[s], L)


def _mlstm_stream(q_ref, k_ref, v_ref, o_ref, g_ref, bi_ref, bf_ref, nw_ref, h_ref, c_ref, n_ref, m_ref, L):
    gates = g_ref[...]
    ig = gates + bi_ref[...]
    b = _cumsum_rows(_log_sigmoid(gates + bf_ref[...]))
    ig_t, b_t = ig.T, b.T
    causal = _iota2((L, L), 0) >= _iota2((L, L), 1)
    lane = _iota2((1, NH_A), 1)
    m_old = m_ref[...]
    m_next = m_old
    for h in range(NH_A):
        q = q_ref[:, h * DK_A:(h + 1) * DK_A]
        k = k_ref[:, h * DK_A:(h + 1) * DK_A] * (DK_A ** -0.5)
        v = v_ref[:, h * DV_A:(h + 1) * DV_A]
        b_c = b[:, NH_A + h:NH_A + h + 1]
        ig_c = ig[:, h:h + 1]
        row = ig_t[h:h + 1, :] - b_t[NH_A + h:NH_A + h + 1, :]
        m_prev = m_old[:, h:h + 1]
        dmat = jnp.where(causal, b_c + row, -jnp.inf)
        inter = b_c + m_prev
        m_s = jnp.maximum(inter, jnp.max(dmat, axis=1, keepdims=True))
        scores = _mm_nt(q, k) * jnp.exp(dmat - m_s)
        w_inter = jnp.exp(inter - m_s)
        num = _mm(scores, v) + w_inter * _mm(q, c_ref[h])
        den = (jnp.sum(scores, axis=1, keepdims=True)
               + w_inter * jnp.sum(q * n_ref[h:h + 1, :], axis=1, keepdims=True))
        hh = num / jnp.maximum(jnp.abs(den), jnp.exp(-m_s))
        hh = _rms(hh, nw_ref[h:h + 1, :])
        gated = _sigmoid(o_ref[:, h * DV_A:(h + 1) * DV_A]) * hh
        h_ref[:, h * DV_A:(h + 1) * DV_A] = gated.astype(h_ref.dtype)
        b_last = b_c[L - 1:L, :]
        dl = b_last - b_c + ig_c
        m_new = jnp.maximum(b_last + m_prev, jnp.max(dl, axis=0, keepdims=True))
        kw = k * jnp.exp(dl - m_new)
        decay = jnp.exp(b_last + m_prev - m_new)
        c_ref[h] = decay * c_ref[h] + _mm_tn(kw, v)
        n_ref[h:h + 1, :] = decay * n_ref[h:h + 1, :] + jnp.sum(kw, axis=0, keepdims=True)
        m_next = jnp.where(lane == h, m_new, m_next)
    m_ref[...] = m_next


def _mlstm(q, k, v, o, gates, bi_row, bf_row, norm_w, c0, n0, m0, L):
    bsz, T, _ = q.shape
    m0 = m0.reshape(bsz, 1, NH_A)
    row128 = _full_spec((1, LANES))
    seq = lambda n: pl.BlockSpec((bsz, L, n), lambda c: (0, c, 0))
    h, c1, n1, m1 = pl.pallas_call(
        functools.partial(_mlstm_kernel, L=L, bsz=bsz),
        grid=(T // L,),
        in_specs=[seq(NH_A * DK_A), seq(NH_A * DK_A), seq(D_A), seq(D_A),
                  seq(LANES), row128, row128, _full_spec((NH_A, DV_A)),
                  _full_spec(c0.shape), _full_spec(n0.shape), _full_spec(m0.shape)],
        out_specs=[seq(D_A), _full_spec(c0.shape), _full_spec(n0.shape), _full_spec(m0.shape)],
        out_shape=[jax.ShapeDtypeStruct((bsz, T, D_A), _MXU_DTYPE),
                   jax.ShapeDtypeStruct(c0.shape, _F32), jax.ShapeDtypeStruct(n0.shape, _F32),
                   jax.ShapeDtypeStruct(m0.shape, _F32)],
        compiler_params=_params(("arbitrary",)),
    )(q, k, v, o, gates, bi_row, bf_row, norm_w, c0, n0, m0)
    return h, c1, n1, m1.reshape(bsz, NH_A)


def _ssd_kernel(x_ref, z_ref, g_ref, cw_ref, cb_ref, dtb_ref, alog_ref, dsk_ref, nw_ref, s0_ref, buf0_ref,
                y_ref, s_ref, tail_ref, win_ref, *, L):
    @pl.when(pl.program_id(1) == 0)
    def _():
        s_ref[...] = s0_ref[...]

    act = _conv_silu(x_ref, cw_ref, cb_ref[...], buf0_ref, win_ref, tail_ref, L)
    dt = _softplus(g_ref[...] + dtb_ref[...])
    b = _cumsum_rows(-jnp.exp(alog_ref[...]) * dt)
    eb = jnp.exp(b)
    b_t = b.T
    causal = _iota2((L, L), 0) >= _iota2((L, L), 1)
    lane0 = 2 * NH_A
    for g in range(NG_B):
        bm = act[:, D_B + g * D_STATE:D_B + (g + 1) * D_STATE]
        cm = act[:, D_B + (NG_B + g) * D_STATE:D_B + (NG_B + g + 1) * D_STATE]
        cb = _mm_nt(cm, bm)
        s_g = s_ref[g * HPG_B:(g + 1) * HPG_B].reshape(HPG_B * HEADDIM_B, D_STATE)
        y_inter = _mm_nt(cm, s_g)
        ys, xws = [], []
        for j in range(HPG_B):
            hd = g * HPG_B + j
            ln = lane0 + hd
            x = act[:, hd * HEADDIM_B:(hd + 1) * HEADDIM_B]
            b_c = b[:, ln:ln + 1]
            dt_c = dt[:, ln:ln + 1]
            seg = jnp.exp(jnp.where(causal, b_c - b_t[ln:ln + 1, :], -jnp.inf))
            y = _mm(cb * seg, dt_c * x) + eb[:, ln:ln + 1] * y_inter[:, j * HEADDIM_B:(j + 1) * HEADDIM_B]
            ys.append(y + dsk_ref[:, hd * HEADDIM_B:(hd + 1) * HEADDIM_B] * x)
            b_last = b_c[L - 1:L, :]
            xws.append((jnp.exp(b_last - b_c) * dt_c) * x)
        upd = _mm_tn(jnp.concatenate(xws, axis=1), bm)
        for j in range(HPG_B):
            hd = g * HPG_B + j
            ln = lane0 + hd
            s_ref[hd] = eb[L - 1:L, ln:ln + 1] * s_ref[hd] + upd[j * HEADDIM_B:(j + 1) * HEADDIM_B, :]
        half = D_B // NG_B
        yg = jnp.concatenate(ys, axis=1) * _silu(z_ref[:, g * half:(g + 1) * half])
        y_ref[:, g * half:(g + 1) * half] = _rms(yg, nw_ref[:, g * half:(g + 1) * half]).astype(y_ref.dtype)


def _ssd(xbc, z, gates, conv_w, conv_b, dtb_row, alog_row, dskip_row, norm_w, s0, buf0, L):
    bsz, T, _ = xbc.shape
    row128 = _full_spec((1, LANES))
    return pl.pallas_call(
        functools.partial(_ssd_kernel, L=L),
        grid=(bsz, T // L),
        in_specs=[_seq_spec(L, CONV_DIM_B), _seq_spec(L, D_B), _seq_spec(L, LANES),
                  _full_spec((CONV_W, CONV_DIM_B)), _full_spec((1, CONV_DIM_B)), row128, row128,
                  _full_spec((1, D_B)), _full_spec((1, D_B)),
                  _state_spec(s0.shape), _state_spec(buf0.shape)],
        out_specs=[_seq_spec(L, D_B), _state_spec(s0.shape), _state_spec(buf0.shape)],
        out_shape=[jax.ShapeDtypeStruct((bsz, T, D_B), _MXU_DTYPE), jax.ShapeDtypeStruct(s0.shape, _F32),
                   jax.ShapeDtypeStruct(buf0.shape, _F32)],
        scratch_shapes=[pltpu.VMEM((CONV_PAD + L, CONV_DIM_B), _F32)],
        compiler_params=_params(("parallel", "arbitrary")),
    )(xbc, z, gates, conv_w, conv_b, dtb_row, alog_row, dskip_row, norm_w, s0, buf0)


def _split_lo(a):
    hi = a.astype(_MXU_DTYPE)
    lo = (a - hi.astype(_F32)).astype(_MXU_DTYPE)
    return hi, lo


def _bmm(a, b):
    return jnp.einsum("hij,hjk->hik", a.astype(_MXU_DTYPE), b.astype(_MXU_DTYPE), preferred_element_type=_F32)


def _bmm_nt(a, b):
    return jnp.einsum("hik,hjk->hij", a.astype(_MXU_DTYPE), b.astype(_MXU_DTYPE), preferred_element_type=_F32)


def _bmm_split(a, b):
    (ah, al), (bh, bl) = a, b
    return _bmm(ah, bh) + _bmm(ah, bl) + _bmm(al, bh)


def _unit_lower_inverse(a, L):
    r, c = _iota2((1, L, L), 1), _iota2((1, L, L), 2)
    eye = (r == c).astype(_F32)
    blk = min(INV_BLOCK, L)
    shift = int(math.log2(blk))
    d = jnp.where((r >> shift) == (c >> shift), a, 0.0)
    x = eye - d
    p = _split_lo(d)
    for _ in range(shift - 1):
        p = _split_lo(_bmm_split(p, p))
        x = x + _bmm_split(_split_lo(x), p)
    while (1 << shift) < L:
        off = jnp.where(((r >> (shift + 1)) == (c >> (shift + 1))) & ((r >> shift) != (c >> shift)), a, 0.0)
        xs = _split_lo(x)
        x = x - _bmm_split(_split_lo(_bmm_split(xs, _split_lo(off))), xs)
        shift += 1
    return x


def _gdn_kernel(x_ref, z_ref, g_ref, cw_ref, dtb_ref, alog_ref, nw_ref, s0_ref, buf0_ref,
                o_ref, s_ref, tail_ref, win_ref, *, L, bsz):
    lo = CONV_PAD - (CONV_W - 1)

    @pl.when(pl.program_id(0) == 0)
    def _():
        s_ref[...] = s0_ref[...]
        win_ref[:, lo:CONV_PAD, :] = buf0_ref[...]

    win_ref[:, CONV_PAD:CONV_PAD + L, :] = x_ref[...]
    y = cw_ref[0:1, :] * win_ref[:, lo:lo + L, :]
    for i in range(1, CONV_W):
        y = y + cw_ref[i:i + 1, :] * win_ref[:, lo + i:lo + i + L, :]
    tail = win_ref[:, L + lo:L + CONV_PAD, :]
    win_ref[:, lo:CONV_PAD, :] = tail
    tail_ref[...] = tail
    act = _silu(y)

    pairs = [(b, h) for b in range(bsz) for h in range(NH_C)]
    heads = lambda a, off, w: jnp.stack([a[b, :, off + h * w:off + (h + 1) * w] for b, h in pairs])
    q, k, v = heads(act, 0, DK_C), heads(act, D_CK, DK_C), heads(act, 2 * D_CK, DV_C)
    q = q * (lax.rsqrt(jnp.sum(q * q, axis=-1, keepdims=True) + EPS) * (DK_C ** -0.5))
    k = k * lax.rsqrt(jnp.sum(k * k, axis=-1, keepdims=True) + EPS)
    gates = jnp.concatenate([g_ref[b] for b in range(bsz)], axis=1)
    tile = lambda row: jnp.concatenate([row] * bsz, axis=1)
    beta = _sigmoid(gates)
    gam = _cumsum_rows(-jnp.exp(tile(alog_ref[...])) * _softplus(gates + tile(dtb_ref[...])))
    gam_t = gam.T
    col = lambda a, off: jnp.stack([a[:, b * LANES + off + h:b * LANES + off + h + 1] for b, h in pairs])
    g_c = col(gam, NH_C)
    g_r = jnp.stack([gam_t[b * LANES + NH_C + h:b * LANES + NH_C + h + 1, :] for b, h in pairs])
    beta_c = col(beta, 0)
    eg_c = jnp.exp(g_c)
    r, c = _iota2((1, L, L), 1), _iota2((1, L, L), 2)
    dec = jnp.exp(jnp.where(r >= c, g_c - g_r, -jnp.inf))
    kb = k.astype(_MXU_DTYPE)
    a_mat = jnp.where(r > c, beta_c * dec * _bmm_nt(kb, kb), 0.0)
    th, tl = _split_lo(_unit_lower_inverse(a_mat, L))
    rhs = jnp.concatenate([beta_c * v, (beta_c * eg_c) * k], axis=-1).astype(_MXU_DTYPE)
    sol = _bmm(th, rhs) + _bmm(tl, rhs)
    s = s_ref[...].reshape(bsz * NH_C, DK_C, DV_C)
    sb = s.astype(_MXU_DTYPE)
    u = sol[:, :, :DV_C] - _bmm(sol[:, :, DV_C:], sb)
    o = eg_c * _bmm(q, sb) + _bmm(_bmm_nt(q, kb) * dec, u)
    g_last = g_c[:, L - 1:L, :]
    kw = jnp.exp(g_last - g_c) * k
    o = _rms(o, nw_ref[...]) * _silu(heads(z_ref[...], 0, DV_C))
    for i, (b, h) in enumerate(pairs):
        s_ref[b, h] = jnp.exp(g_last[i]) * s[i] + _mm_tn(kw[i], u[i])
        o_ref[b, :, h * DV_C:(h + 1) * DV_C] = o[i].astype(o_ref.dtype)


def _gdn(qkv, z, gates, conv_w, dtb_row, alog_row, norm_w, s0, buf0, L):
    bsz, T, _ = qkv.shape
    row128 = _full_spec((1, LANES))
    seq = lambda n: pl.BlockSpec((bsz, L, n), lambda c: (0, c, 0))
    return pl.pallas_call(
        functools.partial(_gdn_kernel, L=L, bsz=bsz),
        grid=(T // L,),
        in_specs=[seq(CONV_DIM_C), seq(D_CV), seq(LANES),
                  _full_spec((CONV_W, CONV_DIM_C)), row128, row128, _full_spec((1, DV_C)),
                  _full_spec(s0.shape), _full_spec(buf0.shape)],
        out_specs=[seq(D_CV), _full_spec(s0.shape), _full_spec(buf0.shape)],
        out_shape=[jax.ShapeDtypeStruct((bsz, T, D_CV), _MXU_DTYPE), jax.ShapeDtypeStruct(s0.shape, _F32),
                   jax.ShapeDtypeStruct(buf0.shape, _F32)],
        scratch_shapes=[pltpu.VMEM((bsz, CONV_PAD + L, CONV_DIM_C), _F32)],
        compiler_params=_params(("arbitrary",)),
    )(qkv, z, gates, conv_w, dtb_row, alog_row, norm_w, s0, buf0)


def _lane_row(vals, offset):
    return jnp.zeros((1, LANES), _F32).at[0, offset:offset + vals.shape[0]].set(vals.astype(_F32))


def _split(w, sizes):
    out, off = [], 0
    for s in sizes:
        out.append(w[:, off:off + s])
        off += s
    return out


def _gate_cols(cols):
    w = jnp.concatenate(cols, axis=1)
    return jnp.pad(w, ((0, 0), (0, LANES - w.shape[1])))


def _prepare(norm_g, norm_f, ffn_w_gate, ffn_w_up, ffn_w_down, ab_w_in, ab_w_out, mlstm_b_i, mlstm_b_f,
             mlstm_norm, ssd_conv_w, ssd_conv_b, ssd_dt_bias, ssd_a_log, ssd_d, ssd_norm, gdn_w_in, gdn_w_out,
             gdn_conv_w, gdn_dt_bias, gdn_a_log, gdn_norm):
    lo = lambda w: w.astype(_MXU_DTYPE)
    wq, wk, wv, wo, wig, wfg, wz, wxbc, wdt = _split(ab_w_in, SIZES_AB)
    wqkv, wzc, wb, wa = _split(gdn_w_in, SIZES_C)
    return dict(
        norm_g=norm_g, norm_f=norm_f,
        wg=lo(ffn_w_gate), wu=lo(ffn_w_up), wd=lo(ffn_w_down),
        ab_in=[lo(w) for w in (wq, wk, wv, wo, wz, wxbc, _gate_cols([wig, wfg, wdt]))],
        ab_out=[lo(ab_w_out[:D_A]), lo(ab_w_out[D_A:])],
        bi_row=_lane_row(mlstm_b_i, 0), bf_row=_lane_row(mlstm_b_f, NH_A),
        mlstm_norm=mlstm_norm.astype(_F32),
        ssd_conv_w=ssd_conv_w, ssd_conv_b=ssd_conv_b.reshape(1, CONV_DIM_B),
        ssd_dtb=_lane_row(ssd_dt_bias, 2 * NH_A), ssd_alog=_lane_row(ssd_a_log, 2 * NH_A),
        ssd_dskip=jnp.repeat(ssd_d.astype(_F32), HEADDIM_B).reshape(1, D_B),
        ssd_norm=ssd_norm.reshape(1, D_B),
        c_in=[lo(w) for w in (wqkv, wzc, _gate_cols([wb, wa]))],
        c_out=lo(gdn_w_out),
        gdn_conv_w=gdn_conv_w,
        gdn_dtb=_lane_row(gdn_dt_bias, NH_C), gdn_alog=_lane_row(gdn_a_log, NH_C),
        gdn_norm=gdn_norm.reshape(1, DV_C),
    )


def _trunk(x, ab_state, c_state, p):
    bsz, T, _ = x.shape
    la, lb, lc = min(T, CHUNK_MLSTM), min(T, CHUNK_SSD), min(T, CHUNK_GDN)
    m = bsz * T
    flat = lambda a: a.reshape(m, a.shape[-1])
    seq = lambda a: a.reshape(bsz, T, a.shape[-1])
    c0, n0, m0, s0, bufb0 = ab_state
    sc0, bufc0 = c_state
    x = flat(x)
    ffn = lambda x, l, i, **kw: _ffn(x, p["norm_g"][l, 2 * i], p["wg"], p["wu"], p["wd"], l, i, **kw)
    x = ffn(x, 0, 0)
    q, k, v, o, z, xbc, gates = _norm_proj(x, p["norm_g"][0, 1], p["ab_in"])
    h_a, c1, n1, m1 = _mlstm(seq(q), seq(k), seq(v), seq(o), seq(gates), p["bi_row"], p["bf_row"],
                             p["mlstm_norm"], c0, n0, m0, la)
    y_b, s1, bufb1 = _ssd(seq(xbc), seq(z), seq(gates), p["ssd_conv_w"], p["ssd_conv_b"], p["ssd_dtb"],
                          p["ssd_alog"], p["ssd_dskip"], p["ssd_norm"], s0, bufb0, lb)
    x = ffn(x, 0, 1, pre=([flat(h_a), flat(y_b)], p["ab_out"]))
    x = ffn(x, 1, 0)
    qkv, zc, gates_c = _norm_proj(x, p["norm_g"][1, 1], p["c_in"])
    o_c, sc1, bufc1 = _gdn(seq(qkv), seq(zc), seq(gates_c), p["gdn_conv_w"], p["gdn_dtb"], p["gdn_alog"],
                           p["gdn_norm"], sc0, bufc0, lc)
    y = ffn(x, 1, 1, final_g=p["norm_f"], pre=([flat(o_c)], [p["c_out"]]))
    return seq(y), (c1, n1, m1, s1, bufb1), (sc1, bufc1)


def kernel(x_prompt, x_sample, state_mlstm_C, state_mlstm_n, state_mlstm_m, state_ssd, cache_ssd_conv, state_gdn, cache_gdn_conv, norm_g, norm_f, ffn_w_gate, ffn_w_up, ffn_w_down, ab_w_in, ab_w_out, mlstm_b_i, mlstm_b_f, mlstm_norm, ssd_conv_w, ssd_conv_b, ssd_dt_bias, ssd_a_log, ssd_d, ssd_norm, gdn_w_in, gdn_w_out, gdn_conv_w, gdn_dt_bias, gdn_a_log, gdn_norm):
    p = _prepare(norm_g, norm_f, ffn_w_gate, ffn_w_up, ffn_w_down, ab_w_in, ab_w_out, mlstm_b_i, mlstm_b_f,
                 mlstm_norm, ssd_conv_w, ssd_conv_b, ssd_dt_bias, ssd_a_log, ssd_d, ssd_norm, gdn_w_in,
                 gdn_w_out, gdn_conv_w, gdn_dt_bias, gdn_a_log, gdn_norm)
    bp = x_prompt.shape[0]
    zeros = lambda *s: jnp.zeros(s, _F32)
    ab0 = (zeros(bp, NH_A, DK_A, DV_A), zeros(bp, NH_A, DK_A), zeros(bp, NH_A),
           zeros(bp, NH_B, HEADDIM_B, D_STATE), zeros(bp, CONV_W - 1, CONV_DIM_B))
    c0 = (zeros(bp, NH_C, DK_C, DV_C), zeros(bp, CONV_W - 1, CONV_DIM_C))
    y_p, ab_p, c_p = _trunk(x_prompt, ab0, c0, p)
    y_s, ab_s, c_s = _trunk(x_sample, (state_mlstm_C, state_mlstm_n, state_mlstm_m, state_ssd, cache_ssd_conv),
                            (state_gdn, cache_gdn_conv), p)
    return (y_p, y_s, *ab_p, *c_p, *ab_s, *c_s)
```
